```python
import jax, jax.numpy as jnp
from jax import lax
import numpy as np

D_MODEL = 1024
BATCH = 16
SEQ = 256
DEPTH = 4
DEC_BATCH = 4
DEC_SEQ = 2048
PAST_LEN = 256

GRID_W = 64
D_RNN = 1024
N_LRU_HEADS = 4
LRU_HEAD_DIM = D_RNN // N_LRU_HEADS
LRU_C = 8.0
CONV_W = 4
CONV_LEFT = 2
CONV_RIGHT = 1
D_FNET = 512
N_FNET_GROUPS = 8
FNET_GROUP_DIM = D_FNET // N_FNET_GROUPS
N_IN = 2 * D_RNN + D_FNET + 2 * D_MODEL
D_FF = ((8 * D_MODEL + 3 * 256 - 1) // (3 * 256)) * 256
EPS = 1e-6

kernel_name = "hybrid_rglru_fnet_diffusion_step"


def rmsnorm(x, g):
    xf = x.astype(jnp.float32)
    y = xf * lax.rsqrt(jnp.mean(xf * xf, axis=-1, keepdims=True) + EPS)
    return (y * g.astype(jnp.float32)).astype(x.dtype)


def adaln_params(c_silu, ada_w, ada_b):
    mod = c_silu @ ada_w + ada_b
    return [m[:, None, :] for m in jnp.split(mod, 6, axis=-1)]


def depthwise_conv(x, w, b):
    L = x.shape[1]
    xp = jnp.pad(x, ((0, 0), (CONV_LEFT, CONV_RIGHT), (0, 0)))
    y = b
    for k in range(CONV_W):
        y = y + xp[:, k:k + L, :] * w[k]
    return y


def _lin_combine(e1, e2):
    a1, b1 = e1
    a2, b2 = e2
    return a1 * a2, a2 * b1 + b2


def rglru_direction(x, wa, ba, wx, bx, lam, h0, reverse):
    B, L, _ = x.shape
    xh = x.reshape(B, L, N_LRU_HEADS, LRU_HEAD_DIM)
    r = jax.nn.sigmoid(jnp.einsum('blhd,hde->blhe', xh, wa).reshape(B, L, D_RNN) + ba)
    i = jax.nn.sigmoid(jnp.einsum('blhd,hde->blhe', xh, wx).reshape(B, L, D_RNN) + bx)
    log_a = -LRU_C * r * jax.nn.softplus(-lam)
    a = jnp.exp(log_a)
    mult = jnp.sqrt(-jnp.expm1(2.0 * log_a))
    b = mult * (i * x)
    if reverse:
        b = b.at[:, -1].add(a[:, -1] * h0)
    else:
        b = b.at[:, 0].add(a[:, 0] * h0)
    _, h = lax.associative_scan(_lin_combine, (a, b), axis=1, reverse=reverse)
    return h


def rglru_bidir(x, wa, ba, wx, bx, lam, h0_f, h0_b):
    xf = x.astype(jnp.float32)
    h_f = rglru_direction(xf, wa[0].astype(jnp.float32), ba[0].astype(jnp.float32), wx[0].astype(jnp.float32),
                          bx[0].astype(jnp.float32), lam[0].astype(jnp.float32), h0_f.astype(jnp.float32), False)
    h_b = rglru_direction(xf, wa[1].astype(jnp.float32), ba[1].astype(jnp.float32), wx[1].astype(jnp.float32),
                          bx[1].astype(jnp.float32), lam[1].astype(jnp.float32), h0_b.astype(jnp.float32), True)
    out = (h_f + h_b).astype(x.dtype)
    return out, h_f[:, -1].astype(x.dtype), h_b[:, 0].astype(x.dtype)


def fourier_mix(xf, on_grid):
    B, L, _ = xf.shape
    xg = xf.astype(jnp.float32).reshape(B, L, N_FNET_GROUPS, FNET_GROUP_DIM)
    if on_grid:
        rows = L // GRID_W
        xg = xg.reshape(B, rows, GRID_W, N_FNET_GROUPS, FNET_GROUP_DIM)
        f = jnp.fft.fftn(xg, axes=(1, 2, 4), norm="ortho").real
    else:
        f = jnp.fft.fftn(xg, axes=(1, 3), norm="ortho").real
    return f.reshape(B, L, D_FNET).astype(xf.dtype)


def block(x, mod, h0_f, h0_b, on_grid, norm1_g, norm2_g, w_in, b_in, conv_w, conv_b, lru_wa, lru_ba, lru_wx,
          lru_bx, lru_lambda, w_lru_out, w_fnet_out, w_out, ffn_w_in, ffn_w_out):
    shift1, scale1, gate1, shift2, scale2, gate2 = mod
    h = rmsnorm(x, norm1_g) * (1.0 + scale1) + shift1
    z = h @ w_in + b_in
    x_r = z[..., :D_RNN]
    y_r = z[..., D_RNN:2 * D_RNN]
    x_f = z[..., 2 * D_RNN:2 * D_RNN + D_FNET]
    g = jax.nn.sigmoid(z[..., 2 * D_RNN + D_FNET:])
    g_a = g[..., :D_MODEL]
    g_b = g[..., D_MODEL:]
    x_r = depthwise_conv(x_r, conv_w, conv_b)
    rec, hf_last, hb_first = rglru_bidir(x_r, lru_wa, lru_ba, lru_wx, lru_bx, lru_lambda, h0_f, h0_b)
    out_a = (rec * jax.nn.gelu(y_r)) @ w_lru_out
    out_b = fourier_mix(x_f, on_grid) @ w_fnet_out
    mixed = (g_a * out_a + g_b * out_b) @ w_out
    x = x + gate1 * mixed
    h2 = rmsnorm(x, norm2_g) * (1.0 + scale2) + shift2
    uv = h2 @ ffn_w_in
    u = uv[..., :D_FF]
    v = uv[..., D_FF:]
    x = x + gate2 * ((jax.nn.silu(u) * v) @ ffn_w_out)
    return x, hf_last, hb_first


def setup_inputs(seed: int = 0) -> dict:
    key = jax.random.key(seed)
    ks = jax.random.split(key, 26)
    f32 = jnp.float32
    nrm = lambda k, s, sc: jax.random.normal(k, s, f32) * sc
    u = jax.random.uniform(ks[15], (DEPTH, 2, D_RNN), f32, minval=0.9, maxval=0.999)
    a0 = u ** (1.0 / LRU_C)
    lru_lambda = jnp.log(a0) - jnp.log1p(-a0)
    return {
        "x_prompt": nrm(ks[0], (BATCH, SEQ, D_MODEL), 1.0),
        "x_sample": nrm(ks[1], (DEC_BATCH, DEC_SEQ, D_MODEL), 1.0),
        "state_lru": nrm(ks[2], (DEC_BATCH, DEPTH, 2, D_RNN), 0.5),
        "c": nrm(ks[3], (DEC_BATCH, D_MODEL), 1.0),
        "c_ctx": nrm(ks[4], (D_MODEL,), 1.0),
        "norm1_g": 1.0 + nrm(ks[5], (DEPTH, D_MODEL), 0.02),
        "norm2_g": 1.0 + nrm(ks[6], (DEPTH, D_MODEL), 0.02),
        "ada_w": nrm(ks[7], (DEPTH, D_MODEL, 6 * D_MODEL), 0.02),
        "ada_b": nrm(ks[8], (DEPTH, 6 * D_MODEL), 0.02),
        "w_in": nrm(ks[9], (DEPTH, D_MODEL, N_IN), D_MODEL ** -0.5),
        "b_in": nrm(ks[10], (DEPTH, N_IN), 0.02),
        "conv_w": nrm(ks[11], (DEPTH, CONV_W, D_RNN), CONV_W ** -0.5),
        "conv_b": nrm(ks[12], (DEPTH, D_RNN), 0.02),
        "lru_wa": nrm(ks[13], (DEPTH, 2, N_LRU_HEADS, LRU_HEAD_DIM, LRU_HEAD_DIM), LRU_HEAD_DIM ** -0.5),
        "lru_ba": nrm(ks[14], (DEPTH, 2, D_RNN), 0.02),
        "lru_wx": nrm(ks[16], (DEPTH, 2, N_LRU_HEADS, LRU_HEAD_DIM, LRU_HEAD_DIM), LRU_HEAD_DIM ** -0.5),
        "lru_bx": nrm(ks[17], (DEPTH, 2, D_RNN), 0.02),
        "lru_lambda": lru_lambda,
        "w_lru_out": nrm(ks[18], (DEPTH, D_RNN, D_MODEL), D_RNN ** -0.5),
        "w_fnet_out": nrm(ks[19], (DEPTH, D_FNET, D_MODEL), D_FNET ** -0.5),
        "w_out": nrm(ks[20], (DEPTH, D_MODEL, D_MODEL), D_MODEL ** -0.5),
        "ffn_w_in": nrm(ks[21], (DEPTH, D_MODEL, 2 * D_FF), D_MODEL ** -0.5),
        "ffn_w_out": nrm(ks[22], (DEPTH, D_FF, D_MODEL), D_FF ** -0.5),
        "final_g": 1.0 + nrm(ks[23], (D_MODEL,), 0.02),
    }


def reference(x_prompt, x_sample, state_lru, c, c_ctx, norm1_g, norm2_g, ada_w, ada_b, w_in, b_in, conv_w, conv_b,
              lru_wa, lru_ba, lru_wx, lru_bx, lru_lambda, w_lru_out, w_fnet_out, w_out, ffn_w_in, ffn_w_out,
              final_g):
    xp = x_prompt
    xs = x_sample
    B = x_prompt.shape[0]
    zeros = jnp.zeros((B, D_RNN), x_prompt.dtype)
    c_ctx_silu = jax.nn.silu(c_ctx)[None, :]
    c_silu = jax.nn.silu(c)
    layer_states = []
    for l in range(DEPTH):
        lp = (norm1_g[l], norm2_g[l], w_in[l], b_in[l], conv_w[l], conv_b[l], lru_wa[l], lru_ba[l], lru_wx[l],
              lru_bx[l], lru_lambda[l], w_lru_out[l], w_fnet_out[l], w_out[l], ffn_w_in[l], ffn_w_out[l])
        mod_ctx = adaln_params(c_ctx_silu, ada_w[l], ada_b[l])
        mod_lat = adaln_params(c_silu, ada_w[l], ada_b[l])
        xp, hf_last, hb_first = block(xp, mod_ctx, zeros, zeros, False, *lp)
        layer_states.append(jnp.stack([hf_last, hb_first], axis=1))
        xs, _, _ = block(xs, mod_lat, state_lru[:, l, 0], state_lru[:, l, 1], True, *lp)
    new_state_lru = jnp.stack(layer_states, axis=1)
    y_prompt = rmsnorm(xp, final_g)
    y_sample = rmsnorm(xs, final_g)
    return (y_prompt, y_sample, new_state_lru)
```

```python
import functools

import numpy as np
import jax
import jax.numpy as jnp
from jax import lax
from jax.experimental import pallas as pl
from jax.experimental.pallas import tpu as pltpu

F32 = jnp.float32
BF16 = jnp.bfloat16

GRID_W = 64
N_LRU_HEADS = 4
LRU_C = 8.0
CONV_W = 4
CONV_LEFT = 2
N_FNET_GROUPS = 8
EPS = 1e-6

SUBLANES = 8
MXU_DIM = 256
VMEM_LIMIT = 56 * 1024 * 1024


def _sigmoid(x):
    return 0.5 * jnp.tanh(0.5 * x) + 0.5


def _dot(a, b):
    return jnp.dot(a, b, preferred_element_type=F32)


def _params(sem, vmem=VMEM_LIMIT):
    return pltpu.CompilerParams(dimension_semantics=sem, vmem_limit_bytes=vmem)


def _resident(block_shape, index_map):
    return pl.BlockSpec(block_shape, index_map, pipeline_mode=pl.Buffered(1))


def _modulate(y, scale_ref, shift_ref):
    rows, d = y.shape
    y3 = y.reshape(rows // SUBLANES, SUBLANES, d)
    y3 = y3 * (1.0 + scale_ref[...])[None] + shift_ref[...][None]
    return y3.reshape(rows, d)


def _gated_add(x, gate_ref, upd):
    rows, d = x.shape
    x3 = x.reshape(rows // SUBLANES, SUBLANES, d)
    u3 = upd.reshape(rows // SUBLANES, SUBLANES, d)
    return (x3 + gate_ref[...][None] * u3).reshape(rows, d)


def _rmsnorm(x, g_ref):
    ms = jnp.mean(x * x, axis=-1, keepdims=True)
    return x * lax.rsqrt(ms + EPS) * g_ref[...]


def _ada_kernel(c_ref, w_ref, b_ref, o_ref):
    c = c_ref[...]
    cs = c * _sigmoid(c)
    o_ref[...] = _dot(cs.astype(BF16), w_ref[...].astype(BF16)) + b_ref[...]


def _ada_call(c16, ada_w, ada_b):
    depth, d, n6 = ada_w.shape
    tn = 1536
    return pl.pallas_call(
        _ada_kernel,
        grid=(depth, n6 // tn),
        in_specs=[
            pl.BlockSpec((2 * SUBLANES, d), lambda l, j: (0, 0)),
            pl.BlockSpec((None, d, tn), lambda l, j: (l, 0, j)),
            pl.BlockSpec((None, 1, tn), lambda l, j: (l, 0, j)),
        ],
        out_specs=pl.BlockSpec((None, 2 * SUBLANES, tn), lambda l, j: (l, 0, j)),
        out_shape=jax.ShapeDtypeStruct((depth, 2 * SUBLANES, n6), F32),
        compiler_params=_params(("arbitrary", "arbitrary")),
        name="ada_mod",
    )(c16, ada_w, ada_b.reshape(depth, 1, n6))


def _inproj_kernel(d_rnn, d_fnet, x_ref, sh_ref, sc_ref, g_ref, w_ref, b_ref, wc_ref,
                   xr_ref, gy_ref, uv_ref, gate_ref):
    h = _modulate(_rmsnorm(x_ref[...], g_ref), sc_ref, sh_ref).astype(BF16)
    o1, o2, o3 = d_rnn, 2 * d_rnn, 2 * d_rnn + d_fnet
    n_in = w_ref.shape[1]
    xr_ref[...] = _dot(h, w_ref[:, 0:o1]) + b_ref[:, 0:o1]
    gy_ref[...] = jax.nn.gelu(_dot(h, w_ref[:, o1:o2]) + b_ref[:, o1:o2])
    xf = _dot(h, w_ref[:, o2:o3]) + b_ref[:, o2:o3]
    uv_ref[...] = _dot(xf.astype(BF16), wc_ref[...]).astype(BF16)
    gate_ref[...] = _sigmoid(_dot(h, w_ref[:, o3:n_in]) + b_ref[:, o3:n_in])


def _inproj_call(l, x, mod, norm_g, w_in, b_in, wc, tm, lat_tiles, d_rnn, d_fnet):
    rows, d = x.shape
    depth, _, n_in = w_in.shape
    n_gate = n_in - 2 * d_rnn - d_fnet
    grp = lambda i: jnp.where(i < lat_tiles, 0, 1)
    return pl.pallas_call(
        functools.partial(_inproj_kernel, d_rnn, d_fnet),
        grid=(rows // tm,),
        in_specs=[
            pl.BlockSpec((tm, d), lambda i: (i, 0)),
            pl.BlockSpec((None, SUBLANES, d), lambda i: (l, grp(i), 0)),
            pl.BlockSpec((None, SUBLANES, d), lambda i: (l, grp(i), 1)),
            pl.BlockSpec((None, 1, d), lambda i: (l, 0, 0)),
            _resident((None, d, n_in), lambda i: (l, 0, 0)),
            pl.BlockSpec((None, 1, n_in), lambda i: (l, 0, 0)),
            _resident((d_fnet, 2 * d_fnet), lambda i: (0, 0)),
        ],
        out_specs=[
            pl.BlockSpec((tm, d_rnn), lambda i: (i, 0)),
            pl.BlockSpec((tm, d_rnn), lambda i: (i, 0)),
            pl.BlockSpec((tm, 2 * d_fnet), lambda i: (i, 0)),
            pl.BlockSpec((tm, n_gate), lambda i: (i, 0)),
        ],
        out_shape=[
            jax.ShapeDtypeStruct((rows, d_rnn), F32),
            jax.ShapeDtypeStruct((rows, d_rnn), F32),
            jax.ShapeDtypeStruct((rows, 2 * d_fnet), BF16),
            jax.ShapeDtypeStruct((rows, n_gate), F32),
        ],
        compiler_params=_params(("arbitrary",)),
        name="in_proj",
    )(x, mod, mod, norm_g, w_in, b_in, wc)


def _lru_kernel(batch, tc, has_h0, *refs):
    if has_h0:
        (x_ref, cw_ref, cb_ref, wg_ref, ba_ref, bx_ref, lam_ref, h0_ref,
         out_ref, win, a_s, b_s) = refs
        st_ref = None
    else:
        (x_ref, cw_ref, cb_ref, wg_ref, ba_ref, bx_ref, lam_ref, _alias,
         out_ref, st_ref, win, a_s, b_s) = refs
        h0_ref = None
    rows, hd = x_ref.shape
    n_chunks = rows // tc
    lead = CONV_LEFT * batch
    tail = (CONV_W - 1 - CONV_LEFT) * batch

    def chunk_ab(c, d):
        r0 = pl.multiple_of(c * tc, tc)
        win[lead:lead + tc, :] = x_ref[pl.ds(r0, tc), :]

        @pl.when(c > 0)
        def _():
            win[0:lead, :] = x_ref[pl.ds(pl.multiple_of(r0 - lead, SUBLANES), lead), :]

        @pl.when(c == 0)
        def _():
            win[0:lead, :] = jnp.zeros((lead, hd), F32)

        @pl.when(c < n_chunks - 1)
        def _():
            win[lead + tc:lead + tc + tail, :] = x_ref[pl.ds(r0 + tc, tail), :]

        @pl.when(c == n_chunks - 1)
        def _():
            win[lead + tc:lead + tc + tail, :] = jnp.zeros((tail, hd), F32)

        xc = cb_ref[...]
        for k in range(CONV_W):
            xc = xc + win[k * batch:k * batch + tc, :] * cw_ref[k:k + 1, :]
        g = _dot(xc.astype(BF16), wg_ref[d])
        r = _sigmoid(g[:, 0:hd] + ba_ref[d])
        i = _sigmoid(g[:, hd:2 * hd] + bx_ref[d])
        z = -lam_ref[d]
        softplus = jnp.maximum(z, 0.0) + jnp.log1p(jnp.exp(-jnp.abs(z)))
        log_a = r * (-LRU_C * softplus)
        a = jnp.exp(log_a)
        mult = jnp.sqrt(-jnp.tanh(log_a) * (1.0 + a * a))
        a_s[...] = a
        b_s[...] = mult * (i * xc)

    if batch == 2 * SUBLANES:
        steps = tc // batch

        def scan_chunk(c, h, reverse):
            r0 = c * tc
            order = range(steps - 1, -1, -1) if reverse else range(steps)
            for t in order:
                sl = slice(t * batch, (t + 1) * batch)
                h = a_s[sl, :] * h + b_s[sl, :]
                dst = pl.ds(pl.multiple_of(r0 + t * batch, batch), batch)
                if reverse:
                    out_ref[dst, :] = out_ref[dst, :] + h
                else:
                    out_ref[dst, :] = h
            return h

        h_init = jnp.zeros((batch, hd), F32)
    else:
        assert 2 * batch == SUBLANES
        steps = tc // SUBLANES
        first_half = lax.broadcasted_iota(jnp.int32, (SUBLANES, hd), 0) < batch

        def scan_chunk(c, h, reverse):
            r0 = c * tc
            early = jnp.logical_not(first_half) if reverse else first_half
            order = range(steps - 1, -1, -1) if reverse else range(steps)
            for j in order:
                sl = slice(j * SUBLANES, (j + 1) * SUBLANES)
                a = a_s[sl, :]
                b = b_s[sl, :]
                ra = pltpu.roll(a, batch, 0)
                rb = pltpu.roll(b, batch, 0)
                a_ra = a * ra
                late_b = a * rb + b
                a1 = jnp.where(early, a, a_ra)
                b1 = jnp.where(early, b, late_b)
                b2 = jnp.where(early, ra * b + rb, late_b)
                v = a1 * h + b1
                h = a_ra * h + b2
                dst = pl.ds(pl.multiple_of(r0 + j * SUBLANES, SUBLANES), SUBLANES)
                if reverse:
                    out_ref[dst, :] = out_ref[dst, :] + v
                else:
                    out_ref[dst, :] = v
            return h

        h_init = None

    def fwd_body(c, h):
        chunk_ab(c, 0)
        return scan_chunk(c, h, False)

    def bwd_body(s, h):
        c = n_chunks - 1 - s
        chunk_ab(c, 1)
        return scan_chunk(c, h, True)

    hf = lax.fori_loop(0, n_chunks, fwd_body, h0_ref[0] if has_h0 else h_init)
    hb = lax.fori_loop(0, n_chunks, bwd_body, h0_ref[1] if has_h0 else h_init)
    if st_ref is not None:
        st_ref[0] = hf
        st_ref[1] = hb


def _lru_call(l, group_rows, row_block, batch, xr, conv_w, conv_b, wg, ba, bx, lam, h0, rec_prev, tc):
    rows, d_rnn = xr.shape
    hd = d_rnn // N_LRU_HEADS
    has_h0 = h0 is not None
    in_specs = [
        pl.BlockSpec((group_rows, hd), lambda h: (row_block, h)),
        pl.BlockSpec((None, CONV_W, hd), lambda h: (l, 0, h)),
        pl.BlockSpec((None, 1, hd), lambda h: (l, 0, h)),
        pl.BlockSpec((None, 2, None, hd, 2 * hd), lambda h: (l, 0, h, 0, 0)),
        pl.BlockSpec((None, 2, 1, hd), lambda h: (l, 0, 0, h)),
        pl.BlockSpec((None, 2, 1, hd), lambda h: (l, 0, 0, h)),
        pl.BlockSpec((None, 2, 1, hd), lambda h: (l, 0, 0, h)),
    ]
    args = [xr, conv_w, conv_b, wg, ba, bx, lam]
    out_specs = [pl.BlockSpec((group_rows, hd), lambda h: (row_block, h))]
    out_shape = [jax.ShapeDtypeStruct((rows, d_rnn), F32)]
    aliases = {}
    if has_h0:
        in_specs.append(pl.BlockSpec((None, 2, SUBLANES, hd), lambda h: (l, 0, 0, h)))
        args.append(h0)
    else:
        in_specs.append(pl.BlockSpec(memory_space=pl.ANY))
        args.append(rec_prev)
        aliases = {len(args) - 1: 0}
        out_specs.append(pl.BlockSpec((2, batch, hd), lambda h: (0, 0, h)))
        out_shape.append(jax.ShapeDtypeStruct((2, batch, d_rnn), F32))
    lead = CONV_LEFT * batch
    tail = (CONV_W - 1 - CONV_LEFT) * batch
    win_rows = lead + tc + max(tail, SUBLANES)
    return pl.pallas_call(
        functools.partial(_lru_kernel, batch, tc, has_h0),
        grid=(N_LRU_HEADS,),
        in_specs=in_specs,
        out_specs=out_specs,
        out_shape=out_shape,
        scratch_shapes=[
            pltpu.VMEM((win_rows, hd), F32),
            pltpu.VMEM((tc, hd), F32),
            pltpu.VMEM((tc, hd), F32),
        ],
        input_output_aliases=aliases,
        compiler_params=_params(("arbitrary",)),
        name="lru_lat" if has_h0 else "lru_ctx",
    )(*args)


def _fnet_kernel(d_fnet, uv_ref, c_ref, s_ref, *rest):
    o_ref = rest[-1]
    y = _dot(c_ref[...], uv_ref[:, 0:d_fnet]) - _dot(s_ref[...], uv_ref[:, d_fnet:2 * d_fnet])
    o_ref[...] = y.astype(BF16)


def _fnet_call(uv_view, y_prev, cmat, smat, seq_len, row_block, n_seq, d_fnet, name):
    in_specs = [
        pl.BlockSpec((seq_len, 2 * d_fnet), lambda b: (row_block, b)),
        _resident((seq_len, seq_len), lambda b: (0, 0)),
        _resident((seq_len, seq_len), lambda b: (0, 0)),
    ]
    args = [uv_view, cmat, smat]
    aliases = {}
    if y_prev is not None:
        in_specs.append(pl.BlockSpec(memory_space=pl.ANY))
        args.append(y_prev)
        aliases = {3: 0}
    return pl.pallas_call(
        functools.partial(_fnet_kernel, d_fnet),
        grid=(n_seq,),
        in_specs=in_specs,
        out_specs=pl.BlockSpec((seq_len, d_fnet), lambda b: (row_block, b)),
        out_shape=jax.ShapeDtypeStruct((uv_view.shape[0], uv_view.shape[1] // 2), BF16),
        input_output_aliases=aliases,
        compiler_params=_params(("arbitrary",)),
        name=name,
    )(*args)


def _mix_ffn_kernel(d_ff, x_ref, rec_ref, gy_ref, y_ref, gate_ref, g1_ref, sh2_ref, sc2_ref, g2_ref,
                    n2_ref, wl_ref, wf_ref, wo_ref, w1_ref, w2_ref, o_ref):
    d = x_ref.shape[1]
    out_a = _dot((rec_ref[...] * gy_ref[...]).astype(BF16), wl_ref[...])
    out_b = _dot(y_ref[...], wf_ref[...])
    merged = gate_ref[:, 0:d] * out_a + gate_ref[:, d:2 * d] * out_b
    x1 = _gated_add(x_ref[...], g1_ref, _dot(merged.astype(BF16), wo_ref[...]))
    h2 = _modulate(_rmsnorm(x1, n2_ref), sc2_ref, sh2_ref).astype(BF16)
    acc = jnp.zeros(x1.shape, F32)
    for c0 in range(0, d_ff, MXU_DIM):
        u = _dot(h2, w1_ref[:, c0:c0 + MXU_DIM])
        v = _dot(h2, w1_ref[:, d_ff + c0:d_ff + c0 + MXU_DIM])
        act = (u * _sigmoid(u)) * v
        acc = acc + _dot(act.astype(BF16), w2_ref[c0:c0 + MXU_DIM, :])
    o_ref[...] = _gated_add(x1, g2_ref, acc)


def _mix_ffn_call(l, x, rec, gy, y, gate, mod, norm_g, wl, wf, wo, w1, w2, tm, lat_tiles):
    rows, d = x.shape
    d_rnn = rec.shape[1]
    d_fnet = y.shape[1]
    d_ff = w2.shape[1]
    grp = lambda i: jnp.where(i < lat_tiles, 0, 1)
    tile = lambda w: pl.BlockSpec((tm, w), lambda i: (i, 0))
    modspec = lambda j: pl.BlockSpec((None, SUBLANES, d), lambda i: (l, grp(i), j))
    return pl.pallas_call(
        functools.partial(_mix_ffn_kernel, d_ff),
        grid=(rows // tm,),
        in_specs=[
            tile(d), tile(d_rnn), tile(d_rnn), tile(d_fnet), tile(2 * d),
            modspec(2), modspec(3), modspec(4), modspec(5),
            pl.BlockSpec((None, 1, d), lambda i: (l, 0, 0)),
            _resident((None, d_rnn, d), lambda i: (l, 0, 0)),
            _resident((None, d_fnet, d), lambda i: (l, 0, 0)),
            _resident((None, d, d), lambda i: (l, 0, 0)),
            _resident((None, d, 2 * d_ff), lambda i: (l, 0, 0)),
            _resident((None, d_ff, d), lambda i: (l, 0, 0)),
        ],
        out_specs=tile(d),
        out_shape=jax.ShapeDtypeStruct((rows, d), F32),
        compiler_params=_params(("arbitrary",)),
        name="mix_ffn",
    )(x, rec, gy, y, gate, mod, mod, mod, mod, norm_g, wl, wf, wo, w1, w2)


def _final_kernel(x_ref, g_ref, o_ref):
    o_ref[...] = _rmsnorm(x_ref[...], g_ref)


def _final_call(x_view, final_g, n_seq, seq_len, row_block0, tl, name):
    d = final_g.shape[1]
    nt = seq_len // tl
    return pl.pallas_call(
        _final_kernel,
        grid=(n_seq, nt),
        in_specs=[
            pl.BlockSpec((tl, d), lambda b, i: (row_block0 + i, b)),
            pl.BlockSpec((1, d), lambda b, i: (0, 0)),
        ],
        out_specs=pl.BlockSpec((tl, d), lambda b, i: (b * nt + i, 0)),
        out_shape=jax.ShapeDtypeStruct((n_seq * seq_len, d), F32),
        compiler_params=_params(("arbitrary", "arbitrary")),
        name=name,
    )(x_view, final_g)


def _cos_sin(n):
    k = np.arange(n)
    ang = 2.0 * np.pi * ((k[:, None] * k[None, :]) % n) / n
    return np.cos(ang), np.sin(ang)


def _channel_dft(d_fnet):
    gd = d_fnet // N_FNET_GROUPS
    c, s = _cos_sin(gd)
    eye = np.eye(N_FNET_GROUPS)
    scale = 1.0 / np.sqrt(gd)
    return np.concatenate([np.kron(eye, c), np.kron(eye, s)], axis=1) * scale


def _seq_dft(n):
    c, s = _cos_sin(n)
    scale = 1.0 / np.sqrt(n)
    return c * scale, s * scale


def _grid_dft(n_rows, n_cols):
    cr, sr = _cos_sin(n_rows)
    cw, sw = _cos_sin(n_cols)
    scale = 1.0 / np.sqrt(n_rows * n_cols)
    return (np.kron(cr, cw) - np.kron(sr, sw)) * scale, (np.kron(sr, cw) + np.kron(cr, sw)) * scale


def kernel(x_prompt, x_sample, state_lru, c, c_ctx, norm1_g, norm2_g, ada_w, ada_b, w_in, b_in, conv_w, conv_b,
           lru_wa, lru_ba, lru_wx, lru_bx, lru_lambda, w_lru_out, w_fnet_out, w_out, ffn_w_in, ffn_w_out,
           final_g):
    n_ctx, l_ctx, d = x_prompt.shape
    n_lat, l_lat, _ = x_sample.shape
    depth = w_in.shape[0]
    d_rnn = conv_w.shape[-1]
    d_fnet = w_fnet_out.shape[1]
    n_in = w_in.shape[-1]
    assert n_lat * 2 == SUBLANES and n_ctx == 2 * SUBLANES
    assert l_lat % GRID_W == 0 and n_in == 2 * d_rnn + d_fnet + 2 * d
    rows_lat, rows_ctx = n_lat * l_lat, n_ctx * l_ctx
    rows = rows_lat + rows_ctx
    assert rows_lat % rows_ctx == 0
    ctx_block = rows_lat // rows_ctx

    tm_in, tm_ffn, tc = 512, 256, 512

    x = jnp.concatenate([
        x_sample.transpose(1, 0, 2).reshape(rows_lat, d),
        x_prompt.transpose(1, 0, 2).reshape(rows_ctx, d)], axis=0)

    c16 = jnp.concatenate([c, c, jnp.broadcast_to(c_ctx[None, :], (SUBLANES, d))], axis=0)
    mod = _ada_call(c16, ada_w, ada_b)

    w_in_b = w_in.astype(BF16)
    wg = jnp.concatenate([lru_wa, lru_wx], axis=-1).astype(BF16)
    wl_b, wf_b, wo_b = w_lru_out.astype(BF16), w_fnet_out.astype(BF16), w_out.astype(BF16)
    w1_b, w2_b = ffn_w_in.astype(BF16), ffn_w_out.astype(BF16)
    b_in3 = b_in.reshape(depth, 1, n_in)
    n1 = norm1_g.reshape(depth, 1, d)
    n2 = norm2_g.reshape(depth, 1, d)
    conv_b3 = conv_b.reshape(depth, 1, d_rnn)
    ba4 = lru_ba.reshape(depth, 2, 1, d_rnn)
    bx4 = lru_bx.reshape(depth, 2, 1, d_rnn)
    lam4 = lru_lambda.reshape(depth, 2, 1, d_rnn)
    h0 = state_lru.transpose(1, 2, 0, 3)
    h0 = jnp.concatenate([h0, h0], axis=2)

    wc = jnp.asarray(_channel_dft(d_fnet), F32).astype(BF16)
    c_ctx_m, s_ctx_m = (jnp.asarray(m, F32).astype(BF16) for m in _seq_dft(l_ctx))
    c_lat_m, s_lat_m = (jnp.asarray(m, F32).astype(BF16) for m in _grid_dft(l_lat // GRID_W, GRID_W))

    states = []
    for l in range(depth):
        xr, gy, uv, gate = _inproj_call(l, x, mod, n1, w_in_b, b_in3, wc, tm_in, rows_lat // tm_in, d_rnn, d_fnet)

        rec = _lru_call(l, rows_lat, 0, n_lat, xr, conv_w, conv_b3, wg, ba4, bx4, lam4, h0, None, tc)[0]
        rec, st = _lru_call(l, rows_ctx, ctx_block, n_ctx, xr, conv_w, conv_b3, wg, ba4, bx4, lam4, None, rec, tc)
        states.append(st)

        y = _fnet_call(uv.reshape(-1, n_lat * 2 * d_fnet), None, c_lat_m, s_lat_m, l_lat, 0, n_lat, d_fnet,
                       "fnet_lat")
        y = _fnet_call(uv.reshape(-1, n_ctx * 2 * d_fnet), y.reshape(-1, n_ctx * d_fnet), c_ctx_m, s_ctx_m,
                       l_ctx, ctx_block, n_ctx, d_fnet, "fnet_ctx")
        y = y.reshape(rows, d_fnet)

        x = _mix_ffn_call(l, x, rec, gy, y, gate, mod, n2, wl_b, wf_b, wo_b, w1_b, w2_b, tm_ffn,
                          rows_lat // tm_ffn)

    fg = final_g.reshape(1, d)
    y_sample = _final_call(x.reshape(-1, n_lat * d), fg, n_lat, l_lat, 0, 256, "final_lat")
    y_prompt = _final_call(x.reshape(-1, n_ctx * d), fg, n_ctx, l_ctx, ctx_block, l_ctx, "final_ctx")
    new_state = jnp.stack(states, axis=0).transpose(2, 0, 1, 3)
    return (y_prompt.reshape(n_ctx, l_ctx, d), y_sample.reshape(n_lat, l_lat, d), new_state)
```

```python
import functools

import numpy as np
import jax
import jax.numpy as jnp
from jax import lax
from jax.experimental import pallas as pl
from jax.experimental.pallas import tpu as pltpu

F32 = jnp.float32
BF16 = jnp.bfloat16

GRID_W = 64
N_LRU_HEADS = 4
LRU_C = 8.0
CONV_W = 4
CONV_LEFT = 2
N_FNET_GROUPS = 8
EPS = 1e-6

SUBLANES = 8
MXU_DIM = 256
MOD_ROWS = 128
VMEM_LIMIT = 56 * 1024 * 1024


def _sigmoid(x):
    return 0.5 * jnp.tanh(0.5 * x) + 0.5


def _dot(a, b):
    return jnp.dot(a, b, preferred_element_type=F32)


def _params(sem, vmem=VMEM_LIMIT):
    return pltpu.CompilerParams(dimension_semantics=sem, vmem_limit_bytes=vmem)


def _resident(block_shape, index_map):
    return pl.BlockSpec(block_shape, index_map, pipeline_mode=pl.Buffered(1))


def _modulate(y, scale_ref, shift_ref):
    rows, d = y.shape
    y3 = y.reshape(rows // MOD_ROWS, MOD_ROWS, d)
    y3 = y3 * (1.0 + scale_ref[...])[None] + shift_ref[...][None]
    return y3.reshape(rows, d)


def _gated_add(x, gate_ref, upd):
    rows, d = x.shape
    x3 = x.reshape(rows // MOD_ROWS, MOD_ROWS, d)
    u3 = upd.reshape(rows // MOD_ROWS, MOD_ROWS, d)
    return (x3 + gate_ref[...][None] * u3).reshape(rows, d)


def _rmsnorm(x, g_ref):
    ms = jnp.mean(x * x, axis=-1, keepdims=True)
    return x * lax.rsqrt(ms + EPS) * g_ref[...]


def _ada_kernel(c_ref, w_ref, b_ref, o_ref):
    c = c_ref[...]
    cs = c * _sigmoid(c)
    o_ref[...] = _dot(cs.astype(BF16), w_ref[...].astype(BF16)) + b_ref[...]


def _ada_call(c_pat, ada_w, ada_b):
    depth, d, n6 = ada_w.shape
    rows = c_pat.shape[0]
    tn = 1536
    return pl.pallas_call(
        _ada_kernel,
        grid=(depth, n6 // tn),
        in_specs=[
            pl.BlockSpec((rows, d), lambda l, j: (0, 0)),
            pl.BlockSpec((None, d, tn), lambda l, j: (l, 0, j)),
            pl.BlockSpec((None, 1, tn), lambda l, j: (l, 0, j)),
        ],
        out_specs=pl.BlockSpec((None, rows, tn), lambda l, j: (l, 0, j)),
        out_shape=jax.ShapeDtypeStruct((depth, rows, n6), F32),
        compiler_params=_params(("arbitrary", "arbitrary")),
        name="ada_mod",
    )(c_pat, ada_w, ada_b.reshape(depth, 1, n6))


def _inproj_kernel(d_rnn, d_fnet, x_ref, sh_ref, sc_ref, g_ref, w_ref, b_ref, wc_ref,
                   xr_ref, gy_ref, uv_ref, gate_ref):
    h = _modulate(_rmsnorm(x_ref[...], g_ref), sc_ref, sh_ref).astype(BF16)
    o1, o2, o3 = d_rnn, 2 * d_rnn, 2 * d_rnn + d_fnet
    n_in = w_ref.shape[1]
    xr_ref[...] = _dot(h, w_ref[:, 0:o1]) + b_ref[:, 0:o1]
    gy_ref[...] = jax.nn.gelu(_dot(h, w_ref[:, o1:o2]) + b_ref[:, o1:o2])
    xf = _dot(h, w_ref[:, o2:o3]) + b_ref[:, o2:o3]
    uv_ref[...] = _dot(xf.astype(BF16), wc_ref[...])
    gate_ref[...] = _sigmoid(_dot(h, w_ref[:, o3:n_in]) + b_ref[:, o3:n_in])


def _inproj_call(l, x, mod, norm_g, w_in, b_in, wc, tm, lat_tiles, d_rnn, d_fnet):
    rows, d = x.shape
    depth, _, n_in = w_in.shape
    n_gate = n_in - 2 * d_rnn - d_fnet
    grp = lambda i: jnp.where(i < lat_tiles, 0, 1)
    return pl.pallas_call(
        functools.partial(_inproj_kernel, d_rnn, d_fnet),
        grid=(rows // tm,),
        in_specs=[
            pl.BlockSpec((tm, d), lambda i: (i, 0)),
            pl.BlockSpec((None, MOD_ROWS, d), lambda i: (l, grp(i), 0)),
            pl.BlockSpec((None, MOD_ROWS, d), lambda i: (l, grp(i), 1)),
            pl.BlockSpec((None, 1, d), lambda i: (l, 0, 0)),
            _resident((None, d, n_in), lambda i: (l, 0, 0)),
            pl.BlockSpec((None, 1, n_in), lambda i: (l, 0, 0)),
            _resident((d_fnet, 2 * d_fnet), lambda i: (0, 0)),
        ],
        out_specs=[
            pl.BlockSpec((tm, d_rnn), lambda i: (i, 0)),
            pl.BlockSpec((tm, d_rnn), lambda i: (i, 0)),
            pl.BlockSpec((tm, 2 * d_fnet), lambda i: (i, 0)),
            pl.BlockSpec((tm, n_gate), lambda i: (i, 0)),
        ],
        out_shape=[
            jax.ShapeDtypeStruct((rows, d_rnn), F32),
            jax.ShapeDtypeStruct((rows, d_rnn), F32),
            jax.ShapeDtypeStruct((rows, 2 * d_fnet), F32),
            jax.ShapeDtypeStruct((rows, n_gate), F32),
        ],
        compiler_params=_params(("arbitrary",)),
        name="in_proj",
    )(x, mod, mod, norm_g, w_in, b_in, wc)


def _lru_kernel(ct, aliased, *refs):
    x_ref, cw_ref, cb_ref, wg_ref, ba_ref, bx_ref, lam_ref, h0_ref = refs[:8]
    out_ref, st_ref, a_s, b_s = refs[9:] if aliased else refs[8:]
    lc, nb, _, hd = x_ref.shape
    tile = (nb, SUBLANES, hd)
    flat = nb * SUBLANES
    sub = lax.broadcasted_iota(jnp.int32, tile, 1)

    def from_prev_chunk(v):
        n = v.shape[0] * flat
        r = pltpu.roll(v.reshape(n, hd), 1, 0).reshape(v.shape)
        return jnp.where(sub[None] == 0, 0.0, r)

    def from_next_chunk(v):
        n = v.shape[0] * flat
        r = pltpu.roll(v.reshape(n, hd), n - 1, 0).reshape(v.shape)
        return jnp.where(sub[None] == SUBLANES - 1, 0.0, r)

    def shifted(t0, off):
        lo, hi = t0 + off, t0 + off + ct
        parts = []
        if lo < 0:
            parts.append(from_prev_chunk(x_ref[lc + lo:lc]))
            lo = 0
        parts.append(x_ref[lo:min(hi, lc)])
        if hi > lc:
            parts.append(from_next_chunk(x_ref[0:hi - lc]))
        return parts[0] if len(parts) == 1 else jnp.concatenate(parts, axis=0)

    z = -lam_ref[...]
    neg_c_softplus = -LRU_C * (jnp.maximum(z, 0.0) + jnp.log1p(jnp.exp(-jnp.abs(z))))

    for t0 in range(0, lc, ct):
        xc = cb_ref[...][None, None]
        for k in range(CONV_W):
            xc = xc + shifted(t0, k - CONV_LEFT) * cw_ref[k:k + 1, :][None, None]
        xc2 = xc.reshape(ct * flat, hd)
        g = _dot(xc2.astype(BF16), wg_ref[...])
        for d in range(2):
            r = _sigmoid(g[:, 2 * d * hd:(2 * d + 1) * hd] + ba_ref[d])
            i = _sigmoid(g[:, (2 * d + 1) * hd:(2 * d + 2) * hd] + bx_ref[d])
            log_a = r * neg_c_softplus[d]
            a = jnp.exp(log_a)
            mult = jnp.sqrt(-jnp.tanh(log_a) * (1.0 + a * a))
            a_s[d, t0:t0 + ct] = a.reshape(ct, *tile)
            b_s[d, t0:t0 + ct] = (mult * (i * xc2)).reshape(ct, *tile)

    def scan_body(s, carry):
        hf, pf, hb, pb = carry
        tb = lc - 1 - s
        af, bf = a_s[0, s], b_s[0, s]
        ab, bb = a_s[1, tb], b_s[1, tb]
        hf = af * hf + bf
        pf = af * pf
        hb = ab * hb + bb
        pb = ab * pb
        b_s[0, s] = hf
        a_s[0, s] = pf
        b_s[1, tb] = hb
        a_s[1, tb] = pb
        return hf, pf, hb, pb

    zero, one = jnp.zeros(tile, F32), jnp.ones(tile, F32)
    hf, pf, hb, pb = lax.fori_loop(0, lc, scan_body, (zero, one, zero, one), unroll=8)

    def chain(h_end, p_end, h0, forward):
        shift, edge = (1, 0) if forward else (flat - 1, SUBLANES - 1)
        state = h0
        for _ in range(SUBLANES - 1):
            nxt = pltpu.roll((p_end * state + h_end).reshape(flat, hd), shift, 0).reshape(tile)
            state = jnp.where(sub == edge, h0, nxt)
        return state

    sf = chain(hf, pf, h0_ref[0], True)
    sb = chain(hb, pb, h0_ref[1], False)
    st_ref[0] = pf * sf + hf
    st_ref[1] = pb * sb + hb

    for t0 in range(0, lc, ct):
        sl = slice(t0, t0 + ct)
        out_ref[sl] = (b_s[0, sl] + a_s[0, sl] * sf[None]) + (b_s[1, sl] + a_s[1, sl] * sb[None])


def _lru_call(l, lead_block, lc, nb, xr4, conv_w, conv_b, wg, ba, bx, lam, h0, rec_prev, ct, name):
    n_tiles, n_seq, _, d_rnn = xr4.shape
    hd = d_rnn // N_LRU_HEADS
    blk = (lc, nb, SUBLANES, hd)
    in_specs = [
        pl.BlockSpec(blk, lambda b, h: (lead_block, b, 0, h)),
        pl.BlockSpec((None, CONV_W, hd), lambda b, h: (l, 0, h)),
        pl.BlockSpec((None, 1, hd), lambda b, h: (l, 0, h)),
        pl.BlockSpec((None, None, hd, 4 * hd), lambda b, h: (l, h, 0, 0)),
        pl.BlockSpec((None, 2, 1, hd), lambda b, h: (l, 0, 0, h)),
        pl.BlockSpec((None, 2, 1, hd), lambda b, h: (l, 0, 0, h)),
        pl.BlockSpec((None, 2, 1, hd), lambda b, h: (l, 0, 0, h)),
        pl.BlockSpec((None, 2, nb, SUBLANES, hd), lambda b, h: (l, 0, b, 0, h)),
    ]
    args = [xr4, conv_w, conv_b, wg, ba, bx, lam, h0]
    aliases = {}
    if rec_prev is not None:
        in_specs.append(pl.BlockSpec(memory_space=pl.ANY))
        args.append(rec_prev)
        aliases = {len(args) - 1: 0}
    return pl.pallas_call(
        functools.partial(_lru_kernel, ct, rec_prev is not None),
        grid=(n_seq // nb, N_LRU_HEADS),
        in_specs=in_specs,
        out_specs=[
            pl.BlockSpec(blk, lambda b, h: (lead_block, b, 0, h)),
            pl.BlockSpec((2, nb, SUBLANES, hd), lambda b, h: (0, b, 0, h)),
        ],
        out_shape=[
            jax.ShapeDtypeStruct(xr4.shape, F32),
            jax.ShapeDtypeStruct((2, n_seq, SUBLANES, d_rnn), F32),
        ],
        scratch_shapes=[
            pltpu.VMEM((2, lc, nb, SUBLANES, hd), F32),
            pltpu.VMEM((2, lc, nb, SUBLANES, hd), F32),
        ],
        input_output_aliases=aliases,
        compiler_params=_params(("arbitrary", "arbitrary")),
        name=name,
    )(*args)


def _fnet_kernel(u_ref, v_ref, c_ref, s_ref, *rest):
    o_ref = rest[-1]
    lc, _, width = u_ref.shape
    u = u_ref[...].reshape(lc * SUBLANES, width).astype(BF16)
    v = v_ref[...].reshape(lc * SUBLANES, width).astype(BF16)
    y = _dot(c_ref[...], u) - _dot(s_ref[...], v)
    o_ref[...] = y.reshape(lc, SUBLANES, width)


def _fnet_call(uv4, y_prev, cmat, smat, lc, lead_block, d_fnet, name):
    n_tiles, n_seq, _, _ = uv4.shape
    seq_len = lc * SUBLANES
    n_col = d_fnet // MXU_DIM
    blk = (lc, None, SUBLANES, MXU_DIM)
    in_specs = [
        pl.BlockSpec(blk, lambda b, j: (lead_block, b, 0, j)),
        pl.BlockSpec(blk, lambda b, j: (lead_block, b, 0, n_col + j)),
        _resident((seq_len, seq_len), lambda b, j: (0, 0)),
        _resident((seq_len, seq_len), lambda b, j: (0, 0)),
    ]
    args = [uv4, uv4, cmat, smat]
    aliases = {}
    if y_prev is not None:
        in_specs.append(pl.BlockSpec(memory_space=pl.ANY))
        args.append(y_prev)
        aliases = {4: 0}
    return pl.pallas_call(
        _fnet_kernel,
        grid=(n_seq, n_col),
        in_specs=in_specs,
        out_specs=pl.BlockSpec(blk, lambda b, j: (lead_block, b, 0, j)),
        out_shape=jax.ShapeDtypeStruct((n_tiles, n_seq, SUBLANES, d_fnet), F32),
        input_output_aliases=aliases,
        compiler_params=_params(("arbitrary", "arbitrary")),
        name=name,
    )(*args)


def _mix_ffn_kernel(d_ff, x_ref, rec_ref, gy_ref, y_ref, gate_ref, g1_ref, sh2_ref, sc2_ref, g2_ref,
                    n2_ref, wl_ref, wf_ref, wo_ref, w1_ref, w2_ref, o_ref):
    d = x_ref.shape[1]
    out_a = _dot((rec_ref[...] * gy_ref[...]).astype(BF16), wl_ref[...])
    out_b = _dot(y_ref[...].astype(BF16), wf_ref[...])
    merged = gate_ref[:, 0:d] * out_a + gate_ref[:, d:2 * d] * out_b
    x1 = _gated_add(x_ref[...], g1_ref, _dot(merged.astype(BF16), wo_ref[...]))
    h2 = _modulate(_rmsnorm(x1, n2_ref), sc2_ref, sh2_ref).astype(BF16)
    acc = jnp.zeros(x1.shape, F32)
    for c0 in range(0, d_ff, MXU_DIM):
        u = _dot(h2, w1_ref[:, c0:c0 + MXU_DIM])
        v = _dot(h2, w1_ref[:, d_ff + c0:d_ff + c0 + MXU_DIM])
        act = (u * _sigmoid(u)) * v
        acc = acc + _dot(act.astype(BF16), w2_ref[c0:c0 + MXU_DIM, :])
    o_ref[...] = _gated_add(x1, g2_ref, acc)


def _mix_ffn_call(l, x, rec, gy, y, gate, mod, norm_g, wl, wf, wo, w1, w2, tm, lat_tiles):
    rows, d = x.shape
    d_rnn = rec.shape[1]
    d_fnet = y.shape[1]
    d_ff = w2.shape[1]
    grp = lambda i: jnp.where(i < lat_tiles, 0, 1)
    tile = lambda w: pl.BlockSpec((tm, w), lambda i: (i, 0))
    modspec = lambda j: pl.BlockSpec((None, MOD_ROWS, d), lambda i: (l, grp(i), j))
    return pl.pallas_call(
        functools.partial(_mix_ffn_kernel, d_ff),
        grid=(rows // tm,),
        in_specs=[
            tile(d), tile(d_rnn), tile(d_rnn), tile(d_fnet), tile(2 * d),
            modspec(2), modspec(3), modspec(4), modspec(5),
            pl.BlockSpec((None, 1, d), lambda i: (l, 0, 0)),
            _resident((None, d_rnn, d), lambda i: (l, 0, 0)),
            _resident((None, d_fnet, d), lambda i: (l, 0, 0)),
            _resident((None, d, d), lambda i: (l, 0, 0)),
            _resident((None, d, 2 * d_ff), lambda i: (l, 0, 0)),
            _resident((None, d_ff, d), lambda i: (l, 0, 0)),
        ],
        out_specs=tile(d),
        out_shape=jax.ShapeDtypeStruct((rows, d), F32),
        compiler_params=_params(("arbitrary",)),
        name="mix_ffn",
    )(x, rec, gy, y, gate, mod, mod, mod, mod, norm_g, wl, wf, wo, w1, w2)


def _final_kernel(x_ref, g_ref, o_ref):
    o_ref[...] = _rmsnorm(x_ref[...], g_ref)


def _final_call(x, final_g, tm):
    rows, d = x.shape
    return pl.pallas_call(
        _final_kernel,
        grid=(rows // tm,),
        in_specs=[
            pl.BlockSpec((tm, d), lambda i: (i, 0)),
            pl.BlockSpec((1, d), lambda i: (0, 0)),
        ],
        out_specs=pl.BlockSpec((tm, d), lambda i: (i, 0)),
        out_shape=jax.ShapeDtypeStruct((rows, d), F32),
        compiler_params=_params(("arbitrary",)),
        name="final_norm",
    )(x, final_g)


def _cos_sin(n):
    k = np.arange(n)
    ang = 2.0 * np.pi * ((k[:, None] * k[None, :]) % n) / n
    return np.cos(ang), np.sin(ang)


def _channel_dft(d_fnet):
    gd = d_fnet // N_FNET_GROUPS
    c, s = _cos_sin(gd)
    eye = np.eye(N_FNET_GROUPS)
    scale = 1.0 / np.sqrt(gd)
    return np.concatenate([np.kron(eye, c), np.kron(eye, s)], axis=1) * scale


def _row_order(seq_len):
    j = np.arange(seq_len)
    return (j % SUBLANES) * (seq_len // SUBLANES) + j // SUBLANES


def _seq_dft(n):
    c, s = _cos_sin(n)
    scale = 1.0 / np.sqrt(n)
    p = _row_order(n)
    return (c * scale)[p][:, p], (s * scale)[p][:, p]


def _grid_dft(n_rows, n_cols):
    cr, sr = _cos_sin(n_rows)
    cw, sw = _cos_sin(n_cols)
    scale = 1.0 / np.sqrt(n_rows * n_cols)
    p = _row_order(n_rows * n_cols)
    c = (np.kron(cr, cw) - np.kron(sr, sw)) * scale
    s = (np.kron(sr, cw) + np.kron(cr, sw)) * scale
    return c[p][:, p], s[p][:, p]


def _to_chunk_layout(x):
    b, l, d = x.shape
    return x.reshape(b, SUBLANES, l // SUBLANES, d).transpose(2, 0, 1, 3).reshape(b * l, d)


def _from_chunk_layout(rows, b, l):
    d = rows.shape[-1]
    return rows.reshape(l // SUBLANES, b, SUBLANES, d).transpose(1, 2, 0, 3).reshape(b, l, d)


def kernel(x_prompt, x_sample, state_lru, c, c_ctx, norm1_g, norm2_g, ada_w, ada_b, w_in, b_in, conv_w, conv_b,
           lru_wa, lru_ba, lru_wx, lru_bx, lru_lambda, w_lru_out, w_fnet_out, w_out, ffn_w_in, ffn_w_out,
           final_g):
    n_ctx, l_ctx, d = x_prompt.shape
    n_lat, l_lat, _ = x_sample.shape
    depth = w_in.shape[0]
    d_rnn = conv_w.shape[-1]
    d_fnet = w_fnet_out.shape[1]
    n_in = w_in.shape[-1]
    hd = d_rnn // N_LRU_HEADS
    assert MOD_ROWS % (n_lat * SUBLANES) == 0 and MOD_ROWS % (n_ctx * SUBLANES) == 0
    assert l_lat % GRID_W == 0 and n_in == 2 * d_rnn + d_fnet + 2 * d
    rows_lat, rows_ctx = n_lat * l_lat, n_ctx * l_ctx
    rows = rows_lat + rows_ctx
    assert rows_lat % rows_ctx == 0
    lc_lat, lc_ctx = l_lat // SUBLANES, l_ctx // SUBLANES
    ctx_block = rows_lat // rows_ctx

    tm_in, tm_ffn, tm_fin = 512, 256, 512

    x = jnp.concatenate([_to_chunk_layout(x_sample), _to_chunk_layout(x_prompt)], axis=0)

    c_pat = jnp.concatenate([
        jnp.tile(jnp.repeat(c, SUBLANES, axis=0), (MOD_ROWS // (n_lat * SUBLANES), 1)),
        jnp.broadcast_to(c_ctx[None, :], (MOD_ROWS, d))], axis=0)
    mod = _ada_call(c_pat, ada_w, ada_b)

    w_in_b = w_in.astype(BF16)
    wg = jnp.concatenate([lru_wa[:, 0], lru_wx[:, 0], lru_wa[:, 1], lru_wx[:, 1]], axis=-1).astype(BF16)
    wl_b, wf_b, wo_b = w_lru_out.astype(BF16), w_fnet_out.astype(BF16), w_out.astype(BF16)
    w1_b, w2_b = ffn_w_in.astype(BF16), ffn_w_out.astype(BF16)
    b_in3 = b_in.reshape(depth, 1, n_in)
    n1 = norm1_g.reshape(depth, 1, d)
    n2 = norm2_g.reshape(depth, 1, d)
    conv_b3 = conv_b.reshape(depth, 1, d_rnn)
    ba4 = lru_ba.reshape(depth, 2, 1, d_rnn)
    bx4 = lru_bx.reshape(depth, 2, 1, d_rnn)
    lam4 = lru_lambda.reshape(depth, 2, 1, d_rnn)
    h0_lat = jnp.broadcast_to(state_lru.transpose(1, 2, 0, 3)[:, :, :, None, :],
                              (depth, 2, n_lat, SUBLANES, d_rnn))
    h0_ctx = jnp.zeros((depth, 2, n_ctx, SUBLANES, d_rnn), F32)

    wc = jnp.asarray(_channel_dft(d_fnet), F32).astype(BF16)
    c_ctx_m, s_ctx_m = (jnp.asarray(m, F32).astype(BF16) for m in _seq_dft(l_ctx))
    c_lat_m, s_lat_m = (jnp.asarray(m, F32).astype(BF16) for m in _grid_dft(l_lat // GRID_W, GRID_W))

    as_lat = lambda a: a.reshape(-1, n_lat, SUBLANES, a.shape[-1])
    as_ctx = lambda a: a.reshape(-1, n_ctx, SUBLANES, a.shape[-1])

    states = []
    for l in range(depth):
        xr, gy, uv, gate = _inproj_call(l, x, mod, n1, w_in_b, b_in3, wc, tm_in, rows_lat // tm_in, d_rnn, d_fnet)

        rec, _ = _lru_call(l, 0, lc_lat, 1, as_lat(xr), conv_w, conv_b3, wg, ba4, bx4, lam4, h0_lat, None,
                           64, "lru_lat")
        rec, st = _lru_call(l, ctx_block, lc_ctx, 4, as_ctx(xr), conv_w, conv_b3, wg, ba4, bx4, lam4, h0_ctx,
                            as_ctx(rec), 16, "lru_ctx")
        states.append(jnp.stack([st[0, :, SUBLANES - 1], st[1, :, 0]], axis=1))

        y = _fnet_call(as_lat(uv), None, c_lat_m, s_lat_m, lc_lat, 0, d_fnet, "fnet_lat")
        y = _fnet_call(as_ctx(uv), as_ctx(y), c_ctx_m, s_ctx_m, lc_ctx, ctx_block, d_fnet, "fnet_ctx")

        x = _mix_ffn_call(l, x, rec.reshape(rows, d_rnn), gy, y.reshape(rows, d_fnet), gate, mod, n2,
                          wl_b, wf_b, wo_b, w1_b, w2_b, tm_ffn, rows_lat // tm_ffn)

    xn = _final_call(x, final_g.reshape(1, d), tm_fin)
    y_sample = _from_chunk_layout(xn[:rows_lat], n_lat, l_lat)
    y_prompt = _from_chunk_layout(xn[rows_lat:], n_ctx, l_ctx)
    new_state = jnp.stack(states, axis=1)
    return (y_prompt, y_sample, new_state)
```

```python
import functools

import numpy as np
import jax
import jax.numpy as jnp
from jax import lax
from jax.experimental import pallas as pl
from jax.experimental.pallas import tpu as pltpu

F32 = jnp.float32
BF16 = jnp.bfloat16

GRID_W = 64
N_LRU_HEADS = 4
LRU_C = 8.0
CONV_W = 4
CONV_LEFT = 2
N_FNET_GROUPS = 8
EPS = 1e-6
LOG2_E = 1.4426950408889634

SUBLANES = 8
MXU_DIM = 256
MOD_ROWS = 128
VMEM_LIMIT = 56 * 1024 * 1024


def _sigmoid(x):
    return 0.5 * jnp.tanh(0.5 * x) + 0.5


def _dot(a, b):
    return jnp.dot(a, b, preferred_element_type=F32)


def _params(sem, vmem=VMEM_LIMIT):
    return pltpu.CompilerParams(dimension_semantics=sem, vmem_limit_bytes=vmem)


def _resident(block_shape, index_map):
    return pl.BlockSpec(block_shape, index_map, pipeline_mode=pl.Buffered(1))


def _modulate(y, scale_ref, shift_ref):
    rows, d = y.shape
    y3 = y.reshape(rows // MOD_ROWS, MOD_ROWS, d)
    y3 = y3 * (1.0 + scale_ref[...])[None] + shift_ref[...][None]
    return y3.reshape(rows, d)


def _gated_add(x, gate_ref, upd):
    rows, d = x.shape
    x3 = x.reshape(rows // MOD_ROWS, MOD_ROWS, d)
    u3 = upd.reshape(rows // MOD_ROWS, MOD_ROWS, d)
    return (x3 + gate_ref[...][None] * u3).reshape(rows, d)


def _rmsnorm(x, g_ref):
    ms = jnp.mean(x * x, axis=-1, keepdims=True)
    return x * lax.rsqrt(ms + EPS) * g_ref[...]


def _ada_kernel(c_ref, w_ref, b_ref, o_ref):
    c = c_ref[...]
    cs = c * _sigmoid(c)
    o_ref[...] = _dot(cs.astype(BF16), w_ref[...].astype(BF16)) + b_ref[...]


def _ada_call(c_pat, ada_w, ada_b):
    depth, d, n6 = ada_w.shape
    rows = c_pat.shape[0]
    tn = 1536
    return pl.pallas_call(
        _ada_kernel,
        grid=(depth, n6 // tn),
        in_specs=[
            pl.BlockSpec((rows, d), lambda l, j: (0, 0)),
            pl.BlockSpec((None, d, tn), lambda l, j: (l, 0, j)),
            pl.BlockSpec((None, 1, tn), lambda l, j: (l, 0, j)),
        ],
        out_specs=pl.BlockSpec((None, rows, tn), lambda l, j: (l, 0, j)),
        out_shape=jax.ShapeDtypeStruct((depth, rows, n6), F32),
        compiler_params=_params(("arbitrary", "arbitrary")),
        name="ada_mod",
    )(c_pat, ada_w, ada_b.reshape(depth, 1, n6))


def _inproj_kernel(d_rnn, d_fnet, x_ref, sh_ref, sc_ref, g_ref, w_ref, b_ref, wc_ref,
                   xr_ref, gy_ref, uv_ref, gate_ref):
    h = _modulate(_rmsnorm(x_ref[...], g_ref), sc_ref, sh_ref).astype(BF16)
    o1, o2, o3 = d_rnn, 2 * d_rnn, 2 * d_rnn + d_fnet
    n_in = w_ref.shape[1]
    xr_ref[...] = _dot(h, w_ref[:, 0:o1]) + b_ref[:, 0:o1]
    gy_ref[...] = jax.nn.gelu(_dot(h, w_ref[:, o1:o2]) + b_ref[:, o1:o2])
    xf = _dot(h, w_ref[:, o2:o3]) + b_ref[:, o2:o3]
    uv_ref[...] = _dot(xf.astype(BF16), wc_ref[...])
    gate_ref[...] = _sigmoid(_dot(h, w_ref[:, o3:n_in]) + b_ref[:, o3:n_in]).astype(BF16)


def _inproj_call(l, x, mod, norm_g, w_in, b_in, wc, tm, lat_tiles, d_rnn, d_fnet):
    rows, d = x.shape
    depth, _, n_in = w_in.shape
    n_gate = n_in - 2 * d_rnn - d_fnet
    grp = lambda i: jnp.where(i < lat_tiles, 0, 1)
    return pl.pallas_call(
        functools.partial(_inproj_kernel, d_rnn, d_fnet),
        grid=(rows // tm,),
        in_specs=[
            pl.BlockSpec((tm, d), lambda i: (i, 0)),
            pl.BlockSpec((None, MOD_ROWS, d), lambda i: (l, grp(i), 0)),
            pl.BlockSpec((None, MOD_ROWS, d), lambda i: (l, grp(i), 1)),
            pl.BlockSpec((None, 1, d), lambda i: (l, 0, 0)),
            _resident((None, d, n_in), lambda i: (l, 0, 0)),
            pl.BlockSpec((None, 1, n_in), lambda i: (l, 0, 0)),
            _resident((d_fnet, 2 * d_fnet), lambda i: (0, 0)),
        ],
        out_specs=[
            pl.BlockSpec((tm, d_rnn), lambda i: (i, 0)),
            pl.BlockSpec((tm, d_rnn), lambda i: (i, 0)),
            pl.BlockSpec((tm, 2 * d_fnet), lambda i: (i, 0)),
            pl.BlockSpec((tm, n_gate), lambda i: (i, 0)),
        ],
        out_shape=[
            jax.ShapeDtypeStruct((rows, d_rnn), F32),
            jax.ShapeDtypeStruct((rows, d_rnn), F32),
            jax.ShapeDtypeStruct((rows, 2 * d_fnet), F32),
            jax.ShapeDtypeStruct((rows, n_gate), BF16),
        ],
        compiler_params=_params(("arbitrary",)),
        name="in_proj",
    )(x, mod, mod, norm_g, w_in, b_in, wc)


def _lru_kernel(ct, aliased, *refs):
    x_ref, gy_ref, cw_ref, cb_ref, wg_ref, ba_ref, bx_ref, lam_ref, h0_ref = refs[:9]
    out_ref, st_ref, a_s, b_s = refs[10:] if aliased else refs[9:]
    lc, nb, _, hd = x_ref.shape
    tile = (nb, SUBLANES, hd)
    flat = nb * SUBLANES
    sub = lax.broadcasted_iota(jnp.int32, tile, 1)

    def from_prev_chunk(v):
        n = v.shape[0] * flat
        r = pltpu.roll(v.reshape(n, hd), 1, 0).reshape(v.shape)
        return jnp.where(sub[None] == 0, 0.0, r)

    def from_next_chunk(v):
        n = v.shape[0] * flat
        r = pltpu.roll(v.reshape(n, hd), n - 1, 0).reshape(v.shape)
        return jnp.where(sub[None] == SUBLANES - 1, 0.0, r)

    def shifted(t0, off):
        lo, hi = t0 + off, t0 + off + ct
        parts = []
        if lo < 0:
            parts.append(from_prev_chunk(x_ref[lc + lo:lc]))
            lo = 0
        parts.append(x_ref[lo:min(hi, lc)])
        if hi > lc:
            parts.append(from_next_chunk(x_ref[0:hi - lc]))
        return parts[0] if len(parts) == 1 else jnp.concatenate(parts, axis=0)

    z = -lam_ref[...]
    softplus = jnp.maximum(z, 0.0) + jnp.log1p(jnp.exp(-jnp.abs(z)))
    half_l2 = (-0.5 * LRU_C * LOG2_E) * softplus
    half_ba = 0.5 * ba_ref[...]
    half_bx = 0.5 * bx_ref[...]

    for t0 in range(0, lc, ct):
        xc = cb_ref[...][None, None]
        for k in range(CONV_W):
            xc = xc + shifted(t0, k - CONV_LEFT) * cw_ref[k:k + 1, :][None, None]
        xc2 = xc.reshape(ct * flat, hd)
        half_xc = 0.5 * xc2
        g = _dot(xc2.astype(BF16), wg_ref[...])
        for d in range(2):
            t_r = jnp.tanh(g[:, 2 * d * hd:(2 * d + 1) * hd] + half_ba[d])
            t_i = jnp.tanh(g[:, (2 * d + 1) * hd:(2 * d + 2) * hd] + half_bx[d])
            a = jnp.exp2(half_l2[d] * t_r + half_l2[d])
            m = 1.0 - a * a
            mult = jnp.where(m > 0.0, m * lax.rsqrt(m), 0.0)
            a_s[d, t0:t0 + ct] = a.reshape(ct, *tile)
            b_s[d, t0:t0 + ct] = ((mult * half_xc) * (t_i + 1.0)).reshape(ct, *tile)

    def scan_body(s, carry):
        hf, pf, hb, pb = carry
        tb = lc - 1 - s
        af, bf = a_s[0, s], b_s[0, s]
        ab, bb = a_s[1, tb], b_s[1, tb]
        hf = af * hf + bf
        pf = af * pf
        hb = ab * hb + bb
        pb = ab * pb
        b_s[0, s] = hf
        a_s[0, s] = pf
        b_s[1, tb] = hb
        a_s[1, tb] = pb
        return hf, pf, hb, pb

    zero, one = jnp.zeros(tile, F32), jnp.ones(tile, F32)
    hf, pf, hb, pb = lax.fori_loop(0, lc, scan_body, (zero, one, zero, one), unroll=8)

    def chain(h_end, p_end, h0, forward):
        shift, edge = (1, 0) if forward else (flat - 1, SUBLANES - 1)
        state = h0
        for _ in range(SUBLANES - 1):
            nxt = pltpu.roll((p_end * state + h_end).reshape(flat, hd), shift, 0).reshape(tile)
            state = jnp.where(sub == edge, h0, nxt)
        return state

    sf = chain(hf, pf, h0_ref[0], True)
    sb = chain(hb, pb, h0_ref[1], False)
    st_ref[0] = pf * sf + hf
    st_ref[1] = pb * sb + hb

    for t0 in range(0, lc, ct):
        sl = slice(t0, t0 + ct)
        rec = (b_s[0, sl] + a_s[0, sl] * sf[None]) + (b_s[1, sl] + a_s[1, sl] * sb[None])
        out_ref[sl] = rec * gy_ref[sl]


def _lru_call(l, lead_block, lc, nb, xr4, gy4, conv_w, conv_b, wg, ba, bx, lam, h0, rec_prev, ct, name):
    n_tiles, n_seq, _, d_rnn = xr4.shape
    hd = d_rnn // N_LRU_HEADS
    blk = (lc, nb, SUBLANES, hd)
    in_specs = [
        pl.BlockSpec(blk, lambda b, h: (lead_block, b, 0, h)),
        pl.BlockSpec(blk, lambda b, h: (lead_block, b, 0, h)),
        pl.BlockSpec((None, CONV_W, hd), lambda b, h: (l, 0, h)),
        pl.BlockSpec((None, 1, hd), lambda b, h: (l, 0, h)),
        pl.BlockSpec((None, None, hd, 4 * hd), lambda b, h: (l, h, 0, 0)),
        pl.BlockSpec((None, 2, 1, hd), lambda b, h: (l, 0, 0, h)),
        pl.BlockSpec((None, 2, 1, hd), lambda b, h: (l, 0, 0, h)),
        pl.BlockSpec((None, 2, 1, hd), lambda b, h: (l, 0, 0, h)),
        pl.BlockSpec((None, 2, nb, SUBLANES, hd), lambda b, h: (l, 0, b, 0, h)),
    ]
    args = [xr4, gy4, conv_w, conv_b, wg, ba, bx, lam, h0]
    aliases = {}
    if rec_prev is not None:
        in_specs.append(pl.BlockSpec(memory_space=pl.ANY))
        args.append(rec_prev)
        aliases = {len(args) - 1: 0}
    return pl.pallas_call(
        functools.partial(_lru_kernel, ct, rec_prev is not None),
        grid=(n_seq // nb, N_LRU_HEADS),
        in_specs=in_specs,
        out_specs=[
            pl.BlockSpec(blk, lambda b, h: (lead_block, b, 0, h)),
            pl.BlockSpec((2, nb, SUBLANES, hd), lambda b, h: (0, b, 0, h)),
        ],
        out_shape=[
            jax.ShapeDtypeStruct(xr4.shape, F32),
            jax.ShapeDtypeStruct((2, n_seq, SUBLANES, d_rnn), F32),
        ],
        scratch_shapes=[
            pltpu.VMEM((2, lc, nb, SUBLANES, hd), F32),
            pltpu.VMEM((2, lc, nb, SUBLANES, hd), F32),
        ],
        input_output_aliases=aliases,
        compiler_params=_params(("arbitrary", "arbitrary")),
        name=name,
    )(*args)


def _fnet_kernel(u_ref, v_ref, c_ref, s_ref, *rest):
    o_ref = rest[-1]
    lc, nb, _, width = u_ref.shape
    rows = lc * SUBLANES
    side_by_side = lambda ref: jnp.concatenate(
        [ref[:, s].reshape(rows, width) for s in range(nb)], axis=1).astype(BF16)
    y = _dot(c_ref[...], side_by_side(u_ref)) - _dot(s_ref[...], side_by_side(v_ref))
    for s in range(nb):
        o_ref[:, s] = y[:, s * width:(s + 1) * width].reshape(lc, SUBLANES, width)


def _fnet_call(uv4, y_prev, cmat, smat, lc, nb, lead_block, d_fnet, name):
    n_tiles, n_seq, _, _ = uv4.shape
    seq_len = lc * SUBLANES
    n_col = d_fnet // MXU_DIM
    blk = (lc, nb, SUBLANES, MXU_DIM)
    in_specs = [
        pl.BlockSpec(blk, lambda b, j: (lead_block, b, 0, j)),
        pl.BlockSpec(blk, lambda b, j: (lead_block, b, 0, n_col + j)),
        _resident((seq_len, seq_len), lambda b, j: (0, 0)),
        _resident((seq_len, seq_len), lambda b, j: (0, 0)),
    ]
    args = [uv4, uv4, cmat, smat]
    aliases = {}
    if y_prev is not None:
        in_specs.append(pl.BlockSpec(memory_space=pl.ANY))
        args.append(y_prev)
        aliases = {4: 0}
    return pl.pallas_call(
        _fnet_kernel,
        grid=(n_seq // nb, n_col),
        in_specs=in_specs,
        out_specs=pl.BlockSpec(blk, lambda b, j: (lead_block, b, 0, j)),
        out_shape=jax.ShapeDtypeStruct((n_tiles, n_seq, SUBLANES, d_fnet), F32),
        input_output_aliases=aliases,
        compiler_params=_params(("arbitrary", "arbitrary")),
        name=name,
    )(*args)


def _mix_ffn_kernel(d_ff, x_ref, rec_ref, y_ref, gate_ref, g1_ref, sh2_ref, sc2_ref, g2_ref,
                    n2_ref, wl_ref, wf_ref, wo_ref, w1_ref, w2_ref, o_ref, act_s):
    d = x_ref.shape[1]
    out_a = _dot(rec_ref[...].astype(BF16), wl_ref[...])
    out_b = _dot(y_ref[...].astype(BF16), wf_ref[...])
    merged = gate_ref[:, 0:d] * out_a + gate_ref[:, d:2 * d] * out_b
    x1 = _gated_add(x_ref[...], g1_ref, _dot(merged.astype(BF16), wo_ref[...]))
    h2 = _modulate(_rmsnorm(x1, n2_ref), sc2_ref, sh2_ref).astype(BF16)
    for c0, c1 in _ffn_chunks(d_ff):
        u = _dot(h2, w1_ref[:, c0:c1])
        v = _dot(h2, w1_ref[:, d_ff + c0:d_ff + c1])
        act_s[:, c0:c1] = ((u * _sigmoid(u)) * v).astype(BF16)
    o_ref[...] = _gated_add(x1, g2_ref, _dot(act_s[...], w2_ref[...]))


def _ffn_chunks(d_ff):
    n_tiles = d_ff // MXU_DIM
    half = (n_tiles + 1) // 2 * MXU_DIM
    return [(0, half), (half, d_ff)]


def _mix_ffn_call(l, x, rec, y, gate, mod, norm_g, wl, wf, wo, w1, w2, tm, lat_tiles):
    rows, d = x.shape
    d_rnn = rec.shape[1]
    d_fnet = y.shape[1]
    d_ff = w2.shape[1]
    grp = lambda i: jnp.where(i < lat_tiles, 0, 1)
    tile = lambda w: pl.BlockSpec((tm, w), lambda i: (i, 0))
    modspec = lambda j: pl.BlockSpec((None, MOD_ROWS, d), lambda i: (l, grp(i), j))
    return pl.pallas_call(
        functools.partial(_mix_ffn_kernel, d_ff),
        grid=(rows // tm,),
        in_specs=[
            tile(d), tile(d_rnn), tile(d_fnet), tile(2 * d),
            modspec(2), modspec(3), modspec(4), modspec(5),
            pl.BlockSpec((None, 1, d), lambda i: (l, 0, 0)),
            _resident((None, d_rnn, d), lambda i: (l, 0, 0)),
            _resident((None, d_fnet, d), lambda i: (l, 0, 0)),
            _resident((None, d, d), lambda i: (l, 0, 0)),
            _resident((None, d, 2 * d_ff), lambda i: (l, 0, 0)),
            _resident((None, d_ff, d), lambda i: (l, 0, 0)),
        ],
        out_specs=tile(d),
        out_shape=jax.ShapeDtypeStruct((rows, d), F32),
        scratch_shapes=[pltpu.VMEM((tm, d_ff), BF16)],
        compiler_params=_params(("arbitrary",)),
        name="mix_ffn",
    )(x, rec, y, gate, mod, mod, mod, mod, norm_g, wl, wf, wo, w1, w2)


def _final_kernel(x_ref, g_ref, o_ref):
    o_ref[...] = _rmsnorm(x_ref[...], g_ref)


def _final_call(x, final_g, tm):
    rows, d = x.shape
    return pl.pallas_call(
        _final_kernel,
        grid=(rows // tm,),
        in_specs=[
            pl.BlockSpec((tm, d), lambda i: (i, 0)),
            pl.BlockSpec((1, d), lambda i: (0, 0)),
        ],
        out_specs=pl.BlockSpec((tm, d), lambda i: (i, 0)),
        out_shape=jax.ShapeDtypeStruct((rows, d), F32),
        compiler_params=_params(("arbitrary",)),
        name="final_norm",
    )(x, final_g)


def _cos_sin(n):
    k = np.arange(n)
    ang = 2.0 * np.pi * ((k[:, None] * k[None, :]) % n) / n
    return np.cos(ang), np.sin(ang)


def _channel_dft(d_fnet):
    gd = d_fnet // N_FNET_GROUPS
    c, s = _cos_sin(gd)
    eye = np.eye(N_FNET_GROUPS)
    scale = 1.0 / np.sqrt(gd)
    return np.concatenate([np.kron(eye, c), np.kron(eye, s)], axis=1) * scale


def _row_order(seq_len):
    j = np.arange(seq_len)
    return (j % SUBLANES) * (seq_len // SUBLANES) + j // SUBLANES


def _seq_dft(n):
    c, s = _cos_sin(n)
    scale = 1.0 / np.sqrt(n)
    p = _row_order(n)
    return (c * scale)[p][:, p], (s * scale)[p][:, p]


def _grid_dft(n_rows, n_cols):
    cr, sr = _cos_sin(n_rows)
    cw, sw = _cos_sin(n_cols)
    scale = 1.0 / np.sqrt(n_rows * n_cols)
    p = _row_order(n_rows * n_cols)
    c = (np.kron(cr, cw) - np.kron(sr, sw)) * scale
    s = (np.kron(sr, cw) + np.kron(cr, sw)) * scale
    return c[p][:, p], s[p][:, p]


def _to_chunk_layout(x):
    b, l, d = x.shape
    return x.reshape(b, SUBLANES, l // SUBLANES, d).transpose(2, 0, 1, 3).reshape(b * l, d)


def _from_chunk_layout(rows, b, l):
    d = rows.shape[-1]
    return rows.reshape(l // SUBLANES, b, SUBLANES, d).transpose(1, 2, 0, 3).reshape(b, l, d)


def kernel(x_prompt, x_sample, state_lru, c, c_ctx, norm1_g, norm2_g, ada_w, ada_b, w_in, b_in, conv_w, conv_b,
           lru_wa, lru_ba, lru_wx, lru_bx, lru_lambda, w_lru_out, w_fnet_out, w_out, ffn_w_in, ffn_w_out,
           final_g):
    n_ctx, l_ctx, d = x_prompt.shape
    n_lat, l_lat, _ = x_sample.shape
    depth = w_in.shape[0]
    d_rnn = conv_w.shape[-1]
    d_fnet = w_fnet_out.shape[1]
    n_in = w_in.shape[-1]
    hd = d_rnn // N_LRU_HEADS
    assert MOD_ROWS % (n_lat * SUBLANES) == 0 and MOD_ROWS % (n_ctx * SUBLANES) == 0
    assert l_lat % GRID_W == 0 and n_in == 2 * d_rnn + d_fnet + 2 * d
    rows_lat, rows_ctx = n_lat * l_lat, n_ctx * l_ctx
    rows = rows_lat + rows_ctx
    assert rows_lat % rows_ctx == 0
    lc_lat, lc_ctx = l_lat // SUBLANES, l_ctx // SUBLANES
    ctx_block = rows_lat // rows_ctx

    tm_in, tm_ffn, tm_fin = 512, 512, 512

    x = jnp.concatenate([_to_chunk_layout(x_sample), _to_chunk_layout(x_prompt)], axis=0)

    c_pat = jnp.concatenate([
        jnp.tile(jnp.repeat(c, SUBLANES, axis=0), (MOD_ROWS // (n_lat * SUBLANES), 1)),
        jnp.broadcast_to(c_ctx[None, :], (MOD_ROWS, d))], axis=0)
    mod = _ada_call(c_pat, ada_w, ada_b)

    w_in_b = w_in.astype(BF16)
    wg = (0.5 * jnp.concatenate([lru_wa[:, 0], lru_wx[:, 0], lru_wa[:, 1], lru_wx[:, 1]], axis=-1)).astype(BF16)
    wl_b, wf_b, wo_b = w_lru_out.astype(BF16), w_fnet_out.astype(BF16), w_out.astype(BF16)
    w1_b, w2_b = ffn_w_in.astype(BF16), ffn_w_out.astype(BF16)
    b_in3 = b_in.reshape(depth, 1, n_in)
    n1 = norm1_g.reshape(depth, 1, d)
    n2 = norm2_g.reshape(depth, 1, d)
    conv_b3 = conv_b.reshape(depth, 1, d_rnn)
    ba4 = lru_ba.reshape(depth, 2, 1, d_rnn)
    bx4 = lru_bx.reshape(depth, 2, 1, d_rnn)
    lam4 = lru_lambda.reshape(depth, 2, 1, d_rnn)
    h0_lat = jnp.broadcast_to(state_lru.transpose(1, 2, 0, 3)[:, :, :, None, :],
                              (depth, 2, n_lat, SUBLANES, d_rnn))
    h0_ctx = jnp.zeros((depth, 2, n_ctx, SUBLANES, d_rnn), F32)

    wc = jnp.asarray(_channel_dft(d_fnet), F32).astype(BF16)
    c_ctx_m, s_ctx_m = (jnp.asarray(m, F32).astype(BF16) for m in _seq_dft(l_ctx))
    c_lat_m, s_lat_m = (jnp.asarray(m, F32).astype(BF16) for m in _grid_dft(l_lat // GRID_W, GRID_W))

    as_lat = lambda a: a.reshape(-1, n_lat, SUBLANES, a.shape[-1])
    as_ctx = lambda a: a.reshape(-1, n_ctx, SUBLANES, a.shape[-1])

    states = []
    for l in range(depth):
        xr, gy, uv, gate = _inproj_call(l, x, mod, n1, w_in_b, b_in3, wc, tm_in, rows_lat // tm_in, d_rnn, d_fnet)

        rec, _ = _lru_call(l, 0, lc_lat, 1, as_lat(xr), as_lat(gy), conv_w, conv_b3, wg, ba4, bx4, lam4,
                           h0_lat, None, 64, "lru_lat")
        rec, st = _lru_call(l, ctx_block, lc_ctx, 4, as_ctx(xr), as_ctx(gy), conv_w, conv_b3, wg, ba4, bx4, lam4,
                            h0_ctx, as_ctx(rec), 16, "lru_ctx")
        states.append(jnp.stack([st[0, :, SUBLANES - 1], st[1, :, 0]], axis=1))

        y = _fnet_call(as_lat(uv), None, c_lat_m, s_lat_m, lc_lat, 1, 0, d_fnet, "fnet_lat")
        y = _fnet_call(as_ctx(uv), as_ctx(y), c_ctx_m, s_ctx_m, lc_ctx, 4, ctx_block, d_fnet, "fnet_ctx")

        x = _mix_ffn_call(l, x, rec.reshape(rows, d_rnn), y.reshape(rows, d_fnet), gate, mod, n2,
                          wl_b, wf_b, wo_b, w1_b, w2_b, tm_ffn, rows_lat // tm_ffn)

    xn = _final_call(x, final_g.reshape(1, d), tm_fin)
    y_sample = _from_chunk_layout(xn[:rows_lat], n_lat, l_lat)
    y_prompt = _from_chunk_layout(xn[rows_lat:], n_ctx, l_ctx)
    new_state = jnp.stack(states, axis=1)
    return (y_prompt, y_sample, new_state)
```

```python
import functools

import numpy as np
import jax
import jax.numpy as jnp
from jax import lax
from jax.experimental import pallas as pl
from jax.experimental.pallas import tpu as pltpu

F32 = jnp.float32
BF16 = jnp.bfloat16

GRID_W = 64
N_LRU_HEADS = 4
LRU_C = 8.0
CONV_W = 4
CONV_LEFT = 2
N_FNET_GROUPS = 8
EPS = 1e-6
LOG2_E = 1.4426950408889634

SUBLANES = 8
MXU_DIM = 256
MOD_ROWS = 128
VMEM_LIMIT = 56 * 1024 * 1024


def _sigmoid(x):
    return 0.5 * jnp.tanh(0.5 * x) + 0.5


def _dot(a, b):
    return jnp.dot(a, b, preferred_element_type=F32)


def _params(sem, vmem=VMEM_LIMIT):
    return pltpu.CompilerParams(dimension_semantics=sem, vmem_limit_bytes=vmem)


def _resident(block_shape, index_map):
    return pl.BlockSpec(block_shape, index_map, pipeline_mode=pl.Buffered(1))


def _modulate(y, scale_ref, shift_ref):
    rows, d = y.shape
    y3 = y.reshape(rows // MOD_ROWS, MOD_ROWS, d)
    y3 = y3 * (1.0 + scale_ref[...])[None] + shift_ref[...][None]
    return y3.reshape(rows, d)


def _gated_add(x, gate_ref, upd):
    rows, d = x.shape
    x3 = x.reshape(rows // MOD_ROWS, MOD_ROWS, d)
    u3 = upd.reshape(rows // MOD_ROWS, MOD_ROWS, d)
    return (x3 + gate_ref[...][None] * u3).reshape(rows, d)


def _rmsnorm(x, g_ref):
    ms = jnp.mean(x * x, axis=-1, keepdims=True)
    return x * lax.rsqrt(ms + EPS) * g_ref[...]


def _ada_kernel(c_ref, w_ref, b_ref, o_ref):
    c = c_ref[...]
    cs = c * _sigmoid(c)
    o_ref[...] = _dot(cs.astype(BF16), w_ref[...].astype(BF16)) + b_ref[...]


def _ada_call(c_pat, ada_w, ada_b):
    depth, d, n6 = ada_w.shape
    rows = c_pat.shape[0]
    tn = 3072
    return pl.pallas_call(
        _ada_kernel,
        grid=(depth, n6 // tn),
        in_specs=[
            pl.BlockSpec((rows, d), lambda l, j: (0, 0)),
            pl.BlockSpec((None, d, tn), lambda l, j: (l, 0, j)),
            pl.BlockSpec((None, 1, tn), lambda l, j: (l, 0, j)),
        ],
        out_specs=pl.BlockSpec((None, rows, tn), lambda l, j: (l, 0, j)),
        out_shape=jax.ShapeDtypeStruct((depth, rows, n6), F32),
        compiler_params=_params(("arbitrary", "arbitrary")),
        name="ada_mod",
    )(c_pat, ada_w, ada_b.reshape(depth, 1, n6))


def _inproj_kernel(d_rnn, d_fnet, x_ref, sh_ref, sc_ref, g_ref, w_ref, b_ref, wc_ref,
                   xr_ref, gy_ref, uv_ref, gate_ref):
    h = _modulate(_rmsnorm(x_ref[...], g_ref), sc_ref, sh_ref).astype(BF16)
    o1, o2, o3 = d_rnn, 2 * d_rnn, 2 * d_rnn + d_fnet
    n_in = w_ref.shape[1]
    xr_ref[...] = _dot(h, w_ref[:, 0:o1]) + b_ref[:, 0:o1]
    gy_ref[...] = jax.nn.gelu(_dot(h, w_ref[:, o1:o2]) + b_ref[:, o1:o2])
    xf = _dot(h, w_ref[:, o2:o3]) + b_ref[:, o2:o3]
    uv_ref[...] = _dot(xf.astype(BF16), wc_ref[...])
    gate_ref[...] = _sigmoid(_dot(h, w_ref[:, o3:n_in]) + b_ref[:, o3:n_in]).astype(BF16)


def _inproj_call(l, x, mod, norm_g, w_in, b_in, wc, tm, lat_tiles, d_rnn, d_fnet):
    rows, d = x.shape
    depth, _, n_in = w_in.shape
    n_gate = n_in - 2 * d_rnn - d_fnet
    grp = lambda i: jnp.where(i < lat_tiles, 0, 1)
    return pl.pallas_call(
        functools.partial(_inproj_kernel, d_rnn, d_fnet),
        grid=(rows // tm,),
        in_specs=[
            pl.BlockSpec((tm, d), lambda i: (i, 0)),
            pl.BlockSpec((None, MOD_ROWS, d), lambda i: (l, grp(i), 0)),
            pl.BlockSpec((None, MOD_ROWS, d), lambda i: (l, grp(i), 1)),
            pl.BlockSpec((None, 1, d), lambda i: (l, 0, 0)),
            _resident((None, d, n_in), lambda i: (l, 0, 0)),
            pl.BlockSpec((None, 1, n_in), lambda i: (l, 0, 0)),
            _resident((d_fnet, 2 * d_fnet), lambda i: (0, 0)),
        ],
        out_specs=[
            pl.BlockSpec((tm, d_rnn), lambda i: (i, 0)),
            pl.BlockSpec((tm, d_rnn), lambda i: (i, 0)),
            pl.BlockSpec((tm, 2 * d_fnet), lambda i: (i, 0)),
            pl.BlockSpec((tm, n_gate), lambda i: (i, 0)),
        ],
        out_shape=[
            jax.ShapeDtypeStruct((rows, d_rnn), F32),
            jax.ShapeDtypeStruct((rows, d_rnn), F32),
            jax.ShapeDtypeStruct((rows, 2 * d_fnet), F32),
            jax.ShapeDtypeStruct((rows, n_gate), BF16),
        ],
        compiler_params=_params(("arbitrary",)),
        name="in_proj",
    )(x, mod, mod, norm_g, w_in, b_in, wc)


def _lru_kernel(ct, x_ref, gy_ref, cw_ref, cb_ref, wg_ref, ba_ref, bx_ref, lam_ref, h0_ref,
                out_ref, st_ref, a_s, b_s):
    lc, nb, _, hd = x_ref.shape
    tile = (nb, SUBLANES, hd)
    flat = nb * SUBLANES
    sub = lax.broadcasted_iota(jnp.int32, tile, 1)

    def from_prev_chunk(v):
        n = v.shape[0] * flat
        r = pltpu.roll(v.reshape(n, hd), 1, 0).reshape(v.shape)
        return jnp.where(sub[None] == 0, 0.0, r)

    def from_next_chunk(v):
        n = v.shape[0] * flat
        r = pltpu.roll(v.reshape(n, hd), n - 1, 0).reshape(v.shape)
        return jnp.where(sub[None] == SUBLANES - 1, 0.0, r)

    def shifted(t0, off):
        lo, hi = t0 + off, t0 + off + ct
        parts = []
        if lo < 0:
            parts.append(from_prev_chunk(x_ref[lc + lo:lc]))
            lo = 0
        parts.append(x_ref[lo:min(hi, lc)])
        if hi > lc:
            parts.append(from_next_chunk(x_ref[0:hi - lc]))
        return parts[0] if len(parts) == 1 else jnp.concatenate(parts, axis=0)

    z = -lam_ref[...]
    softplus = jnp.maximum(z, 0.0) + jnp.log1p(jnp.exp(-jnp.abs(z)))
    half_l2 = (-0.5 * LRU_C * LOG2_E) * softplus
    half_ba = 0.5 * ba_ref[...]
    half_bx = 0.5 * bx_ref[...]

    for t0 in range(0, lc, ct):
        xc = cb_ref[...][None, None]
        for k in range(CONV_W):
            xc = xc + shifted(t0, k - CONV_LEFT) * cw_ref[k:k + 1, :][None, None]
        xc2 = xc.reshape(ct * flat, hd)
        half_xc = 0.5 * xc2
        g = _dot(xc2.astype(BF16), wg_ref[...])
        for d in range(2):
            t_r = jnp.tanh(g[:, 2 * d * hd:(2 * d + 1) * hd] + half_ba[d])
            t_i = jnp.tanh(g[:, (2 * d + 1) * hd:(2 * d + 2) * hd] + half_bx[d])
            a = jnp.exp2(half_l2[d] * t_r + half_l2[d])
            m = 1.0 - a * a
            mult = jnp.where(m > 0.0, m * lax.rsqrt(m), 0.0)
            a_s[d, t0:t0 + ct] = a.reshape(ct, *tile)
            b_s[d, t0:t0 + ct] = ((mult * half_xc) * (t_i + 1.0)).reshape(ct, *tile)

    def scan_body(s, carry):
        hf, pf, hb, pb = carry
        tb = lc - 1 - s
        af, bf = a_s[0, s], b_s[0, s]
        ab, bb = a_s[1, tb], b_s[1, tb]
        hf = af * hf + bf
        pf = af * pf
        hb = ab * hb + bb
        pb = ab * pb
        b_s[0, s] = hf
        a_s[0, s] = pf
        b_s[1, tb] = hb
        a_s[1, tb] = pb
        return hf, pf, hb, pb

    zero, one = jnp.zeros(tile, F32), jnp.ones(tile, F32)
    hf, pf, hb, pb = lax.fori_loop(0, lc, scan_body, (zero, one, zero, one), unroll=8)

    def chain(h_end, p_end, h0, forward):
        shift, edge = (1, 0) if forward else (flat - 1, SUBLANES - 1)
        state = h0
        for _ in range(SUBLANES - 1):
            nxt = pltpu.roll((p_end * state + h_end).reshape(flat, hd), shift, 0).reshape(tile)
            state = jnp.where(sub == edge, h0, nxt)
        return state

    sf = chain(hf, pf, h0_ref[0], True)
    sb = chain(hb, pb, h0_ref[1], False)
    st_ref[0] = pf * sf + hf
    st_ref[1] = pb * sb + hb

    for t0 in range(0, lc, ct):
        sl = slice(t0, t0 + ct)
        rec = (b_s[0, sl] + a_s[0, sl] * sf[None]) + (b_s[1, sl] + a_s[1, sl] * sb[None])
        out_ref[sl] = rec * gy_ref[sl]


def _lru_call(l, lead_block, lc, nb, xr4, gy4, conv_w, conv_b, wg, ba, bx, lam, h0, ct, name):
    n_tiles, n_seq, _, d_rnn = xr4.shape
    hd = d_rnn // N_LRU_HEADS
    blk = (lc, nb, SUBLANES, hd)
    in_specs = [
        pl.BlockSpec(blk, lambda b, h: (lead_block, b, 0, h)),
        pl.BlockSpec(blk, lambda b, h: (lead_block, b, 0, h)),
        pl.BlockSpec((None, CONV_W, hd), lambda b, h: (l, 0, h)),
        pl.BlockSpec((None, 1, hd), lambda b, h: (l, 0, h)),
        pl.BlockSpec((None, None, hd, 4 * hd), lambda b, h: (l, h, 0, 0)),
        pl.BlockSpec((None, 2, 1, hd), lambda b, h: (l, 0, 0, h)),
        pl.BlockSpec((None, 2, 1, hd), lambda b, h: (l, 0, 0, h)),
        pl.BlockSpec((None, 2, 1, hd), lambda b, h: (l, 0, 0, h)),
        pl.BlockSpec((None, 2, nb, SUBLANES, hd), lambda b, h: (l, 0, b, 0, h)),
    ]
    return pl.pallas_call(
        functools.partial(_lru_kernel, ct),
        grid=(n_seq // nb, N_LRU_HEADS),
        in_specs=in_specs,
        out_specs=[
            pl.BlockSpec(blk, lambda b, h: (lead_block, b, 0, h)),
            pl.BlockSpec((2, nb, SUBLANES, hd), lambda b, h: (0, b, 0, h)),
        ],
        out_shape=[
            jax.ShapeDtypeStruct(xr4.shape, F32),
            jax.ShapeDtypeStruct((2, n_seq, SUBLANES, d_rnn), F32),
        ],
        scratch_shapes=[
            pltpu.VMEM((2, lc, nb, SUBLANES, hd), F32),
            pltpu.VMEM((2, lc, nb, SUBLANES, hd), F32),
        ],
        input_output_aliases={0: 0},
        compiler_params=_params(("arbitrary", "arbitrary")),
        name=name,
    )(xr4, gy4, conv_w, conv_b, wg, ba, bx, lam, h0)


def _fnet_kernel(uv_ref, c_ref, s_ref, o_ref):
    lc, nb, _, width = o_ref.shape
    rows = lc * SUBLANES
    side_by_side = lambda lo: jnp.concatenate(
        [uv_ref[:, s, :, lo:lo + width].reshape(rows, width) for s in range(nb)], axis=1).astype(BF16)
    y = _dot(c_ref[...], side_by_side(0)) - _dot(s_ref[...], side_by_side(width))
    for s in range(nb):
        o_ref[:, s] = y[:, s * width:(s + 1) * width].reshape(lc, SUBLANES, width)


def _fnet_call(uv4, cmat, smat, lc, nb, lead_block, name):
    n_tiles, n_seq, _, width = uv4.shape
    seq_len = lc * SUBLANES
    return pl.pallas_call(
        _fnet_kernel,
        grid=(n_seq // nb, width // (2 * MXU_DIM)),
        in_specs=[
            pl.BlockSpec((lc, nb, SUBLANES, 2 * MXU_DIM), lambda b, j: (lead_block, b, 0, j)),
            _resident((seq_len, seq_len), lambda b, j: (0, 0)),
            _resident((seq_len, seq_len), lambda b, j: (0, 0)),
        ],
        out_specs=pl.BlockSpec((lc, nb, SUBLANES, MXU_DIM), lambda b, j: (lead_block, b, 0, 2 * j)),
        out_shape=jax.ShapeDtypeStruct(uv4.shape, F32),
        input_output_aliases={0: 0},
        compiler_params=_params(("arbitrary", "arbitrary")),
        name=name,
    )(uv4, cmat, smat)


def _mix_ffn_kernel(d_ff, lat_tiles, final, x_ref, rec_ref, y0_ref, y1_ref, gate_ref, g1_ref, sh2_ref, sc2_ref,
                    g2_ref, n2_ref, wl_ref, wf_ref, wo_ref, w1_ref, w2_ref, *rest):
    fin_ref = rest[0] if final else None
    outs, act_s = rest[1 if final else 0:-1], rest[-1]
    d = x_ref.shape[1]
    out_a = _dot(rec_ref[...].astype(BF16), wl_ref[...])
    out_b = _dot(jnp.concatenate([y0_ref[...], y1_ref[...]], axis=1).astype(BF16), wf_ref[...])
    merged = gate_ref[:, 0:d] * out_a + gate_ref[:, d:2 * d] * out_b
    x1 = _gated_add(x_ref[...], g1_ref, _dot(merged.astype(BF16), wo_ref[...]))
    h2 = _modulate(_rmsnorm(x1, n2_ref), sc2_ref, sh2_ref).astype(BF16)
    for c0, c1 in _ffn_chunks(d_ff):
        u = _dot(h2, w1_ref[:, c0:c1])
        v = _dot(h2, w1_ref[:, d_ff + c0:d_ff + c1])
        act_s[:, c0:c1] = ((u * _sigmoid(u)) * v).astype(BF16)
    x2 = _gated_add(x1, g2_ref, _dot(act_s[...], w2_ref[...]))
    if fin_ref is None:
        outs[0][...] = x2
    else:
        xn = _rmsnorm(x2, fin_ref)
        i = pl.program_id(0)

        @pl.when(i < lat_tiles)
        def _():
            outs[0][...] = xn

        @pl.when(i >= lat_tiles)
        def _():
            outs[1][...] = xn


def _ffn_chunks(d_ff):
    n_tiles = d_ff // MXU_DIM
    half = (n_tiles + 1) // 2 * MXU_DIM
    return [(0, half), (half, d_ff)]


def _mix_ffn_call(l, x, rec, y, gate, mod, norm_g, wl, wf, wo, w1, w2, tm, lat_tiles, final_g):
    rows, d = x.shape
    d_rnn = rec.shape[1]
    d_fnet, d_ff = wf.shape[1], w2.shape[1]
    assert d_fnet == 2 * MXU_DIM
    n_tiles = rows // tm
    grp = lambda i: jnp.where(i < lat_tiles, 0, 1)
    tile = lambda w, j=0: pl.BlockSpec((tm, w), lambda i: (i, j))
    modspec = lambda j: pl.BlockSpec((None, MOD_ROWS, d), lambda i: (l, grp(i), j))
    in_specs = [
        tile(d), tile(d_rnn), tile(MXU_DIM, 0), tile(MXU_DIM, 2), tile(2 * d),
        modspec(2), modspec(3), modspec(4), modspec(5),
        pl.BlockSpec((None, 1, d), lambda i: (l, 0, 0)),
        _resident((None, d_rnn, d), lambda i: (l, 0, 0)),
        _resident((None, d_fnet, d), lambda i: (l, 0, 0)),
        _resident((None, d, d), lambda i: (l, 0, 0)),
        _resident((None, d, 2 * d_ff), lambda i: (l, 0, 0)),
        _resident((None, d_ff, d), lambda i: (l, 0, 0)),
    ]
    args = [x, rec, y, y, gate, mod, mod, mod, mod, norm_g, wl, wf, wo, w1, w2]
    if final_g is None:
        out_specs = tile(d)
        out_shape = jax.ShapeDtypeStruct((rows, d), F32)
    else:
        in_specs.append(pl.BlockSpec((1, d), lambda i: (0, 0)))
        args.append(final_g)
        out_specs = [
            pl.BlockSpec((tm, d), lambda i: (jnp.minimum(i, lat_tiles - 1), 0)),
            pl.BlockSpec((tm, d), lambda i: (jnp.maximum(i - lat_tiles, 0), 0)),
        ]
        out_shape = [
            jax.ShapeDtypeStruct((lat_tiles * tm, d), F32),
            jax.ShapeDtypeStruct(((n_tiles - lat_tiles) * tm, d), F32),
        ]
    return pl.pallas_call(
        functools.partial(_mix_ffn_kernel, d_ff, lat_tiles, final_g is not None),
        grid=(n_tiles,),
        in_specs=in_specs,
        out_specs=out_specs,
        out_shape=out_shape,
        scratch_shapes=[pltpu.VMEM((tm, d_ff), BF16)],
        compiler_params=_params(("arbitrary",)),
        name="mix_ffn",
    )(*args)


def _cos_sin(n):
    k = np.arange(n)
    ang = 2.0 * np.pi * ((k[:, None] * k[None, :]) % n) / n
    return np.cos(ang), np.sin(ang)


def _channel_dft(d_fnet):
    gd = d_fnet // N_FNET_GROUPS
    c, s = _cos_sin(gd)
    eye = np.eye(N_FNET_GROUPS)
    scale = 1.0 / np.sqrt(gd)
    u, v = np.kron(eye, c) * scale, np.kron(eye, s) * scale
    blocks = []
    for j in range(0, d_fnet, MXU_DIM):
        blocks += [u[:, j:j + MXU_DIM], v[:, j:j + MXU_DIM]]
    return np.concatenate(blocks, axis=1)


def _row_order(seq_len):
    j = np.arange(seq_len)
    return (j % SUBLANES) * (seq_len // SUBLANES) + j // SUBLANES


def _seq_dft(n):
    c, s = _cos_sin(n)
    scale = 1.0 / np.sqrt(n)
    p = _row_order(n)
    return (c * scale)[p][:, p], (s * scale)[p][:, p]


def _grid_dft(n_rows, n_cols):
    cr, sr = _cos_sin(n_rows)
    cw, sw = _cos_sin(n_cols)
    scale = 1.0 / np.sqrt(n_rows * n_cols)
    p = _row_order(n_rows * n_cols)
    c = (np.kron(cr, cw) - np.kron(sr, sw)) * scale
    s = (np.kron(sr, cw) + np.kron(cr, sw)) * scale
    return c[p][:, p], s[p][:, p]


def _to_chunk_layout(x):
    b, l, d = x.shape
    return x.reshape(b, SUBLANES, l // SUBLANES, d).transpose(2, 0, 1, 3).reshape(b * l, d)


def _from_chunk_layout(rows, b, l):
    d = rows.shape[-1]
    return rows.reshape(l // SUBLANES, b, SUBLANES, d).transpose(1, 2, 0, 3).reshape(b, l, d)


def kernel(x_prompt, x_sample, state_lru, c, c_ctx, norm1_g, norm2_g, ada_w, ada_b, w_in, b_in, conv_w, conv_b,
           lru_wa, lru_ba, lru_wx, lru_bx, lru_lambda, w_lru_out, w_fnet_out, w_out, ffn_w_in, ffn_w_out,
           final_g):
    n_ctx, l_ctx, d = x_prompt.shape
    n_lat, l_lat, _ = x_sample.shape
    depth = w_in.shape[0]
    d_rnn = conv_w.shape[-1]
    d_fnet = w_fnet_out.shape[1]
    n_in = w_in.shape[-1]
    hd = d_rnn // N_LRU_HEADS
    assert MOD_ROWS % (n_lat * SUBLANES) == 0 and MOD_ROWS % (n_ctx * SUBLANES) == 0
    assert l_lat % GRID_W == 0 and n_in == 2 * d_rnn + d_fnet + 2 * d
    rows_lat, rows_ctx = n_lat * l_lat, n_ctx * l_ctx
    rows = rows_lat + rows_ctx
    assert rows_lat % rows_ctx == 0
    lc_lat, lc_ctx = l_lat // SUBLANES, l_ctx // SUBLANES
    ctx_block = rows_lat // rows_ctx

    tm_in, tm_ffn = 512, 512

    x = jnp.concatenate([_to_chunk_layout(x_sample), _to_chunk_layout(x_prompt)], axis=0)

    c_pat = jnp.concatenate([
        jnp.tile(jnp.repeat(c, SUBLANES, axis=0), (MOD_ROWS // (n_lat * SUBLANES), 1)),
        jnp.broadcast_to(c_ctx[None, :], (MOD_ROWS, d))], axis=0)
    mod = _ada_call(c_pat, ada_w, ada_b)

    w_in_b = w_in.astype(BF16)
    wg = (0.5 * jnp.concatenate([lru_wa[:, 0], lru_wx[:, 0], lru_wa[:, 1], lru_wx[:, 1]], axis=-1)).astype(BF16)
    wl_b, wf_b, wo_b = w_lru_out.astype(BF16), w_fnet_out.astype(BF16), w_out.astype(BF16)
    w1_b, w2_b = ffn_w_in.astype(BF16), ffn_w_out.astype(BF16)
    b_in3 = b_in.reshape(depth, 1, n_in)
    n1 = norm1_g.reshape(depth, 1, d)
    n2 = norm2_g.reshape(depth, 1, d)
    conv_b3 = conv_b.reshape(depth, 1, d_rnn)
    ba4 = lru_ba.reshape(depth, 2, 1, d_rnn)
    bx4 = lru_bx.reshape(depth, 2, 1, d_rnn)
    lam4 = lru_lambda.reshape(depth, 2, 1, d_rnn)
    h0_lat = jnp.broadcast_to(state_lru.transpose(1, 2, 0, 3)[:, :, :, None, :],
                              (depth, 2, n_lat, SUBLANES, d_rnn))
    h0_ctx = jnp.zeros((depth, 2, n_ctx, SUBLANES, d_rnn), F32)

    wc = jnp.asarray(_channel_dft(d_fnet), F32).astype(BF16)
    c_ctx_m, s_ctx_m = (jnp.asarray(m, F32).astype(BF16) for m in _seq_dft(l_ctx))
    c_lat_m, s_lat_m = (jnp.asarray(m, F32).astype(BF16) for m in _grid_dft(l_lat // GRID_W, GRID_W))

    as_lat = lambda a: a.reshape(-1, n_lat, SUBLANES, a.shape[-1])
    as_ctx = lambda a: a.reshape(-1, n_ctx, SUBLANES, a.shape[-1])

    states = []
    for l in range(depth):
        xr, gy, uv, gate = _inproj_call(l, x, mod, n1, w_in_b, b_in3, wc, tm_in, rows_lat // tm_in, d_rnn, d_fnet)

        rec, _ = _lru_call(l, 0, lc_lat, 1, as_lat(xr), as_lat(gy), conv_w, conv_b3, wg, ba4, bx4, lam4,
                           h0_lat, 64, "lru_lat")
        rec, st = _lru_call(l, ctx_block, lc_ctx, 4, as_ctx(rec), as_ctx(gy), conv_w, conv_b3, wg, ba4, bx4, lam4,
                            h0_ctx, 16, "lru_ctx")
        states.append(jnp.stack([st[0, :, SUBLANES - 1], st[1, :, 0]], axis=1))

        y = _fnet_call(as_lat(uv), c_lat_m, s_lat_m, lc_lat, 1, 0, "fnet_lat")
        y = _fnet_call(as_ctx(y), c_ctx_m, s_ctx_m, lc_ctx, 4, ctx_block, "fnet_ctx")

        x = _mix_ffn_call(l, x, rec.reshape(rows, d_rnn), y.reshape(rows, 2 * d_fnet), gate, mod, n2,
                          wl_b, wf_b, wo_b, w1_b, w2_b, tm_ffn, rows_lat // tm_ffn,
                          final_g.reshape(1, d) if l == depth - 1 else None)

    y_sample = _from_chunk_layout(x[0], n_lat, l_lat)
    y_prompt = _from_chunk_layout(x[1], n_ctx, l_ctx)
    new_state = jnp.stack(states, axis=1)
    return (y_prompt, y_sample, new_state)
```

```python
import functools

import numpy as np
import jax
import jax.numpy as jnp
from jax import lax
from jax.experimental import pallas as pl
from jax.experimental.pallas import tpu as pltpu

F32 = jnp.float32
BF16 = jnp.bfloat16

GRID_W = 64
N_LRU_HEADS = 4
LRU_C = 8.0
CONV_W = 4
CONV_LEFT = 2
N_FNET_GROUPS = 8
EPS = 1e-6
LOG2_E = 1.4426950408889634

SUBLANES = 8
MXU_DIM = 256
MOD_ROWS = 128
VMEM_LIMIT = 60000 * 1024


def _sigmoid(x):
    return 0.5 * jnp.tanh(0.5 * x) + 0.5


def _dot(a, b):
    return jnp.dot(a, b, preferred_element_type=F32)


def _params(sem, vmem=VMEM_LIMIT):
    return pltpu.CompilerParams(dimension_semantics=sem, vmem_limit_bytes=vmem)


def _resident(block_shape, index_map):
    return pl.BlockSpec(block_shape, index_map, pipeline_mode=pl.Buffered(1))


def _modulate(y, scale_ref, shift_ref):
    rows, d = y.shape
    y3 = y.reshape(rows // MOD_ROWS, MOD_ROWS, d)
    y3 = y3 * (1.0 + scale_ref[...])[None] + shift_ref[...][None]
    return y3.reshape(rows, d)


def _gated_add(x, gate_ref, upd):
    rows, d = x.shape
    x3 = x.reshape(rows // MOD_ROWS, MOD_ROWS, d)
    u3 = upd.reshape(rows // MOD_ROWS, MOD_ROWS, d)
    return (x3 + gate_ref[...][None] * u3).reshape(rows, d)


def _rmsnorm(x, g_ref):
    ms = jnp.mean(x * x, axis=-1, keepdims=True)
    return x * lax.rsqrt(ms + EPS) * g_ref[...]


def _ada_kernel(c_ref, w_ref, b_ref, o_ref):
    c = c_ref[...]
    cs = c * _sigmoid(c)
    o_ref[...] = _dot(cs.astype(BF16), w_ref[...].astype(BF16)) + b_ref[...]


def _ada_call(c_pat, ada_w, ada_b):
    depth, d, n6 = ada_w.shape
    rows = c_pat.shape[0]
    tn = 3072
    return pl.pallas_call(
        _ada_kernel,
        grid=(depth, n6 // tn),
        in_specs=[
            pl.BlockSpec((rows, d), lambda l, j: (0, 0)),
            pl.BlockSpec((None, d, tn), lambda l, j: (l, 0, j)),
            pl.BlockSpec((None, 1, tn), lambda l, j: (l, 0, j)),
        ],
        out_specs=pl.BlockSpec((None, rows, tn), lambda l, j: (l, 0, j)),
        out_shape=jax.ShapeDtypeStruct((depth, rows, n6), F32),
        compiler_params=_params(("arbitrary", "arbitrary")),
        name="ada_mod",
    )(c_pat, ada_w, ada_b.reshape(depth, 1, n6))


def _inproj_kernel(d_rnn, d_fnet, x_ref, sh_ref, sc_ref, g_ref, w_ref, b_ref, wc_ref,
                   xr_ref, gy_ref, uv_ref, gate_ref):
    h = _modulate(_rmsnorm(x_ref[...], g_ref), sc_ref, sh_ref).astype(BF16)
    o1, o2, o3 = d_rnn, 2 * d_rnn, 2 * d_rnn + d_fnet
    n_in = w_ref.shape[1]
    xr_ref[...] = _dot(h, w_ref[:, 0:o1]) + b_ref[:, 0:o1]
    gy_ref[...] = jax.nn.gelu(_dot(h, w_ref[:, o1:o2]) + b_ref[:, o1:o2])
    xf = _dot(h, w_ref[:, o2:o3]) + b_ref[:, o2:o3]
    uv_ref[...] = _dot(xf.astype(BF16), wc_ref[...])
    gate_ref[...] = _sigmoid(_dot(h, w_ref[:, o3:n_in]) + b_ref[:, o3:n_in]).astype(BF16)


def _inproj_call(l, x, mod, norm_g, w_in, b_in, wc, tm, lat_tiles, d_rnn, d_fnet):
    rows, d = x.shape
    depth, _, n_in = w_in.shape
    n_gate = n_in - 2 * d_rnn - d_fnet
    grp = lambda i: jnp.where(i < lat_tiles, 0, 1)
    return pl.pallas_call(
        functools.partial(_inproj_kernel, d_rnn, d_fnet),
        grid=(rows // tm,),
        in_specs=[
            pl.BlockSpec((tm, d), lambda i: (i, 0)),
            pl.BlockSpec((None, MOD_ROWS, d), lambda i: (l, grp(i), 0)),
            pl.BlockSpec((None, MOD_ROWS, d), lambda i: (l, grp(i), 1)),
            pl.BlockSpec((None, 1, d), lambda i: (l, 0, 0)),
            _resident((None, d, n_in), lambda i: (l, 0, 0)),
            pl.BlockSpec((None, 1, n_in), lambda i: (l, 0, 0)),
            _resident((d_fnet, 2 * d_fnet), lambda i: (0, 0)),
        ],
        out_specs=[
            pl.BlockSpec((tm, d_rnn), lambda i: (i, 0)),
            pl.BlockSpec((tm, d_rnn), lambda i: (i, 0)),
            pl.BlockSpec((tm, 2 * d_fnet), lambda i: (i, 0)),
            pl.BlockSpec((tm, n_gate), lambda i: (i, 0)),
        ],
        out_shape=[
            jax.ShapeDtypeStruct((rows, d_rnn), F32),
            jax.ShapeDtypeStruct((rows, d_rnn), F32),
            jax.ShapeDtypeStruct((rows, 2 * d_fnet), F32),
            jax.ShapeDtypeStruct((rows, n_gate), BF16),
        ],
        compiler_params=_params(("arbitrary",)),
        name="in_proj",
    )(x, mod, mod, norm_g, w_in, b_in, wc)


def _seq_mix_kernel(ct, x_ref, gy_ref, cw_ref, cb_ref, wg_ref, ba_ref, bx_ref, lam_ref, h0_ref,
                    uv_ref, c_ref, s_ref, out_ref, st_ref, y_ref, a_s, b_s):
    lc, nb, _, hd = x_ref.shape
    tile = (nb, SUBLANES, hd)
    flat = nb * SUBLANES
    sub = lax.broadcasted_iota(jnp.int32, tile, 1)

    width = y_ref.shape[-1]
    seq_rows = lc * SUBLANES
    n_chunks = lc // ct
    piece = seq_rows // 2 // n_chunks
    side_by_side = lambda lo: jnp.concatenate(
        [uv_ref[:, s, :, lo:lo + width].reshape(seq_rows, width) for s in range(nb)], axis=1).astype(BF16)
    u_all, v_all = side_by_side(0), side_by_side(width)

    def fourier_piece(k):
        r0 = pl.multiple_of((pl.program_id(1) % 2) * (seq_rows // 2) + k * piece, piece)
        y = _dot(c_ref[pl.ds(r0, piece), :], u_all) - _dot(s_ref[pl.ds(r0, piece), :], v_all)
        tiles = slice(k * piece // SUBLANES, (k + 1) * piece // SUBLANES)
        for s in range(nb):
            y_ref[tiles, s] = y[:, s * width:(s + 1) * width].reshape(piece // SUBLANES, SUBLANES, width)

    def from_prev_chunk(v):
        n = v.shape[0] * flat
        r = pltpu.roll(v.reshape(n, hd), 1, 0).reshape(v.shape)
        return jnp.where(sub[None] == 0, 0.0, r)

    def from_next_chunk(v):
        n = v.shape[0] * flat
        r = pltpu.roll(v.reshape(n, hd), n - 1, 0).reshape(v.shape)
        return jnp.where(sub[None] == SUBLANES - 1, 0.0, r)

    def shifted(t0, off):
        lo, hi = t0 + off, t0 + off + ct
        parts = []
        if lo < 0:
            parts.append(from_prev_chunk(x_ref[lc + lo:lc]))
            lo = 0
        parts.append(x_ref[lo:min(hi, lc)])
        if hi > lc:
            parts.append(from_next_chunk(x_ref[0:hi - lc]))
        return parts[0] if len(parts) == 1 else jnp.concatenate(parts, axis=0)

    z = -lam_ref[...]
    softplus = jnp.maximum(z, 0.0) + jnp.log1p(jnp.exp(-jnp.abs(z)))
    half_l2 = (-0.5 * LRU_C * LOG2_E) * softplus
    half_ba = 0.5 * ba_ref[...]
    half_bx = 0.5 * bx_ref[...]

    for t0 in range(0, lc, ct):
        xc = cb_ref[...][None, None]
        for k in range(CONV_W):
            xc = xc + shifted(t0, k - CONV_LEFT) * cw_ref[k:k + 1, :][None, None]
        xc2 = xc.reshape(ct * flat, hd)
        half_xc = 0.5 * xc2
        g = _dot(xc2.astype(BF16), wg_ref[...])
        fourier_piece(t0 // ct)
        for d in range(2):
            t_r = jnp.tanh(g[:, 2 * d * hd:(2 * d + 1) * hd] + half_ba[d])
            t_i = jnp.tanh(g[:, (2 * d + 1) * hd:(2 * d + 2) * hd] + half_bx[d])
            a = jnp.exp2(half_l2[d] * t_r + half_l2[d])
            m = 1.0 - a * a
            mult = jnp.where(m > 0.0, m * lax.rsqrt(m), 0.0)
            a_s[d, t0:t0 + ct] = a.reshape(ct, *tile)
            b_s[d, t0:t0 + ct] = ((mult * half_xc) * (t_i + 1.0)).reshape(ct, *tile)

    def scan_body(s, carry):
        hf, pf, hb, pb = carry
        tb = lc - 1 - s
        af, bf = a_s[0, s], b_s[0, s]
        ab, bb = a_s[1, tb], b_s[1, tb]
        hf = af * hf + bf
        pf = af * pf
        hb = ab * hb + bb
        pb = ab * pb
        b_s[0, s] = hf
        a_s[0, s] = pf
        b_s[1, tb] = hb
        a_s[1, tb] = pb
        return hf, pf, hb, pb

    zero, one = jnp.zeros(tile, F32), jnp.ones(tile, F32)
    hf, pf, hb, pb = lax.fori_loop(0, lc, scan_body, (zero, one, zero, one), unroll=8)

    def chain(h_end, p_end, h0, forward):
        shift, edge = (1, 0) if forward else (flat - 1, SUBLANES - 1)
        state = h0
        for _ in range(SUBLANES - 1):
            nxt = pltpu.roll((p_end * state + h_end).reshape(flat, hd), shift, 0).reshape(tile)
            state = jnp.where(sub == edge, h0, nxt)
        return state

    sf = chain(hf, pf, h0_ref[0], True)
    sb = chain(hb, pb, h0_ref[1], False)
    st_ref[0] = pf * sf + hf
    st_ref[1] = pb * sb + hb

    for t0 in range(0, lc, ct):
        sl = slice(t0, t0 + ct)
        rec = (b_s[0, sl] + a_s[0, sl] * sf[None]) + (b_s[1, sl] + a_s[1, sl] * sb[None])
        out_ref[sl] = rec * gy_ref[sl]


def _seq_mix_call(l, lead_block, lc, nb, xr4, gy4, uv4, conv_w, conv_b, wg, ba, bx, lam, h0, cmat, smat, ct, name):
    n_tiles, n_seq, _, d_rnn = xr4.shape
    hd = d_rnn // N_LRU_HEADS
    assert N_LRU_HEADS == 2 * (uv4.shape[-1] // (2 * MXU_DIM))
    seq_len = lc * SUBLANES
    blk = (lc, nb, SUBLANES, hd)
    in_specs = [
        pl.BlockSpec(blk, lambda b, h: (lead_block, b, 0, h)),
        pl.BlockSpec(blk, lambda b, h: (lead_block, b, 0, h)),
        pl.BlockSpec((None, CONV_W, hd), lambda b, h: (l, 0, h)),
        pl.BlockSpec((None, 1, hd), lambda b, h: (l, 0, h)),
        pl.BlockSpec((None, None, hd, 4 * hd), lambda b, h: (l, h, 0, 0)),
        pl.BlockSpec((None, 2, 1, hd), lambda b, h: (l, 0, 0, h)),
        pl.BlockSpec((None, 2, 1, hd), lambda b, h: (l, 0, 0, h)),
        pl.BlockSpec((None, 2, 1, hd), lambda b, h: (l, 0, 0, h)),
        pl.BlockSpec((None, 2, nb, SUBLANES, hd), lambda b, h: (l, 0, b, 0, h)),
        pl.BlockSpec((lc, nb, SUBLANES, 2 * MXU_DIM), lambda b, h: (lead_block, b, 0, h // 2)),
        _resident((seq_len, seq_len), lambda b, h: (0, 0)),
        _resident((seq_len, seq_len), lambda b, h: (0, 0)),
    ]
    return pl.pallas_call(
        functools.partial(_seq_mix_kernel, ct),
        grid=(n_seq // nb, N_LRU_HEADS),
        in_specs=in_specs,
        out_specs=[
            pl.BlockSpec(blk, lambda b, h: (lead_block, b, 0, h)),
            pl.BlockSpec((2, nb, SUBLANES, hd), lambda b, h: (0, b, 0, h)),
            pl.BlockSpec((lc // 2, nb, SUBLANES, MXU_DIM), lambda b, h: (h % 2, b, 0, h // 2)),
        ],
        out_shape=[
            jax.ShapeDtypeStruct(xr4.shape, F32),
            jax.ShapeDtypeStruct((2, n_seq, SUBLANES, d_rnn), F32),
            jax.ShapeDtypeStruct((lc, n_seq, SUBLANES, uv4.shape[-1] // 2), F32),
        ],
        scratch_shapes=[
            pltpu.VMEM((2, lc, nb, SUBLANES, hd), F32),
            pltpu.VMEM((2, lc, nb, SUBLANES, hd), F32),
        ],
        input_output_aliases={0: 0},
        compiler_params=_params(("arbitrary", "arbitrary")),
        name=name,
    )(xr4, gy4, conv_w, conv_b, wg, ba, bx, lam, h0, uv4, cmat, smat)


def _mix_ffn_kernel(d_ff, lat_tiles, final, x_ref, rec_ref, ylat_ref, yctx_ref, gate_ref, g1_ref, sh2_ref, sc2_ref,
                    g2_ref, n2_ref, wl_ref, wf_ref, wo_ref, w1_ref, w2_ref, *rest):
    fin_ref = rest[0] if final else None
    outs, act_s = rest[1 if final else 0:-1], rest[-1]
    d = x_ref.shape[1]
    is_lat = pl.program_id(0) < lat_tiles
    out_a = _dot(rec_ref[...].astype(BF16), wl_ref[...])
    out_b = _dot(jnp.where(is_lat, ylat_ref[...], yctx_ref[...]).astype(BF16), wf_ref[...])
    merged = gate_ref[:, 0:d] * out_a + gate_ref[:, d:2 * d] * out_b
    x1 = _gated_add(x_ref[...], g1_ref, _dot(merged.astype(BF16), wo_ref[...]))
    h2 = _modulate(_rmsnorm(x1, n2_ref), sc2_ref, sh2_ref).astype(BF16)
    for c0, c1 in _ffn_chunks(d_ff):
        u = _dot(h2, w1_ref[:, c0:c1])
        v = _dot(h2, w1_ref[:, d_ff + c0:d_ff + c1])
        act_s[:, c0:c1] = ((u * _sigmoid(u)) * v).astype(BF16)
    x2 = _gated_add(x1, g2_ref, _dot(act_s[...], w2_ref[...]))
    if fin_ref is None:
        outs[0][...] = x2
    else:
        xn = _rmsnorm(x2, fin_ref)

        @pl.when(is_lat)
        def _():
            outs[0][...] = xn

        @pl.when(jnp.logical_not(is_lat))
        def _():
            outs[1][...] = xn


def _ffn_chunks(d_ff):
    n_tiles = d_ff // MXU_DIM
    half = (n_tiles + 1) // 2 * MXU_DIM
    return [(0, half), (half, d_ff)]


def _mix_ffn_call(l, x, rec, y_lat, y_ctx, gate, mod, norm_g, wl, wf, wo, w1, w2, tm, lat_tiles, final_g):
    rows, d = x.shape
    d_rnn = rec.shape[1]
    d_fnet, d_ff = wf.shape[1], w2.shape[1]
    n_tiles = rows // tm
    grp = lambda i: jnp.where(i < lat_tiles, 0, 1)
    tile = lambda w: pl.BlockSpec((tm, w), lambda i: (i, 0))
    modspec = lambda j: pl.BlockSpec((None, MOD_ROWS, d), lambda i: (l, grp(i), j))
    lat_blk = lambda i: (jnp.minimum(i, lat_tiles - 1), 0)
    ctx_blk = lambda i: (jnp.maximum(i - lat_tiles, 0), 0)
    in_specs = [
        tile(d), tile(d_rnn), pl.BlockSpec((tm, d_fnet), lat_blk), pl.BlockSpec((tm, d_fnet), ctx_blk), tile(2 * d),
        modspec(2), modspec(3), modspec(4), modspec(5),
        pl.BlockSpec((None, 1, d), lambda i: (l, 0, 0)),
        _resident((None, d_rnn, d), lambda i: (l, 0, 0)),
        _resident((None, d_fnet, d), lambda i: (l, 0, 0)),
        _resident((None, d, d), lambda i: (l, 0, 0)),
        _resident((None, d, 2 * d_ff), lambda i: (l, 0, 0)),
        _resident((None, d_ff, d), lambda i: (l, 0, 0)),
    ]
    args = [x, rec, y_lat, y_ctx, gate, mod, mod, mod, mod, norm_g, wl, wf, wo, w1, w2]
    if final_g is None:
        out_specs = tile(d)
        out_shape = jax.ShapeDtypeStruct((rows, d), F32)
    else:
        in_specs.append(pl.BlockSpec((1, d), lambda i: (0, 0)))
        args.append(final_g)
        out_specs = [pl.BlockSpec((tm, d), lat_blk), pl.BlockSpec((tm, d), ctx_blk)]
        out_shape = [
            jax.ShapeDtypeStruct((lat_tiles * tm, d), F32),
            jax.ShapeDtypeStruct(((n_tiles - lat_tiles) * tm, d), F32),
        ]
    return pl.pallas_call(
        functools.partial(_mix_ffn_kernel, d_ff, lat_tiles, final_g is not None),
        grid=(n_tiles,),
        in_specs=in_specs,
        out_specs=out_specs,
        out_shape=out_shape,
        scratch_shapes=[pltpu.VMEM((tm, d_ff), BF16)],
        compiler_params=_params(("arbitrary",)),
        name="mix_ffn",
    )(*args)


def _cos_sin(n):
    k = np.arange(n)
    ang = 2.0 * np.pi * ((k[:, None] * k[None, :]) % n) / n
    return np.cos(ang), np.sin(ang)


def _channel_dft(d_fnet):
    gd = d_fnet // N_FNET_GROUPS
    c, s = _cos_sin(gd)
    eye = np.eye(N_FNET_GROUPS)
    scale = 1.0 / np.sqrt(gd)
    u, v = np.kron(eye, c) * scale, np.kron(eye, s) * scale
    blocks = []
    for j in range(0, d_fnet, MXU_DIM):
        blocks += [u[:, j:j + MXU_DIM], v[:, j:j + MXU_DIM]]
    return np.concatenate(blocks, axis=1)


def _row_order(seq_len):
    j = np.arange(seq_len)
    return (j % SUBLANES) * (seq_len // SUBLANES) + j // SUBLANES


def _seq_dft(n):
    c, s = _cos_sin(n)
    scale = 1.0 / np.sqrt(n)
    p = _row_order(n)
    return (c * scale)[p][:, p], (s * scale)[p][:, p]


def _grid_dft(n_rows, n_cols):
    cr, sr = _cos_sin(n_rows)
    cw, sw = _cos_sin(n_cols)
    scale = 1.0 / np.sqrt(n_rows * n_cols)
    p = _row_order(n_rows * n_cols)
    c = (np.kron(cr, cw) - np.kron(sr, sw)) * scale
    s = (np.kron(sr, cw) + np.kron(cr, sw)) * scale
    return c[p][:, p], s[p][:, p]


def _to_chunk_layout(x):
    b, l, d = x.shape
    return x.reshape(b, SUBLANES, l // SUBLANES, d).transpose(2, 0, 1, 3).reshape(b * l, d)


def _from_chunk_layout(rows, b, l):
    d = rows.shape[-1]
    return rows.reshape(l // SUBLANES, b, SUBLANES, d).transpose(1, 2, 0, 3).reshape(b, l, d)


def kernel(x_prompt, x_sample, state_lru, c, c_ctx, norm1_g, norm2_g, ada_w, ada_b, w_in, b_in, conv_w, conv_b,
           lru_wa, lru_ba, lru_wx, lru_bx, lru_lambda, w_lru_out, w_fnet_out, w_out, ffn_w_in, ffn_w_out,
           final_g):
    n_ctx, l_ctx, d = x_prompt.shape
    n_lat, l_lat, _ = x_sample.shape
    depth = w_in.shape[0]
    d_rnn = conv_w.shape[-1]
    d_fnet = w_fnet_out.shape[1]
    n_in = w_in.shape[-1]
    hd = d_rnn // N_LRU_HEADS
    assert MOD_ROWS % (n_lat * SUBLANES) == 0 and MOD_ROWS % (n_ctx * SUBLANES) == 0
    assert l_lat % GRID_W == 0 and n_in == 2 * d_rnn + d_fnet + 2 * d
    rows_lat, rows_ctx = n_lat * l_lat, n_ctx * l_ctx
    rows = rows_lat + rows_ctx
    assert rows_lat % rows_ctx == 0
    lc_lat, lc_ctx = l_lat // SUBLANES, l_ctx // SUBLANES
    ctx_block = rows_lat // rows_ctx

    tm_in, tm_ffn = 512, 512

    x = jnp.concatenate([_to_chunk_layout(x_sample), _to_chunk_layout(x_prompt)], axis=0)

    c_pat = jnp.concatenate([
        jnp.tile(jnp.repeat(c, SUBLANES, axis=0), (MOD_ROWS // (n_lat * SUBLANES), 1)),
        jnp.broadcast_to(c_ctx[None, :], (MOD_ROWS, d))], axis=0)
    mod = _ada_call(c_pat, ada_w, ada_b)

    w_in_b = w_in.astype(BF16)
    wg = (0.5 * jnp.concatenate([lru_wa[:, 0], lru_wx[:, 0], lru_wa[:, 1], lru_wx[:, 1]], axis=-1)).astype(BF16)
    wl_b, wf_b, wo_b = w_lru_out.astype(BF16), w_fnet_out.astype(BF16), w_out.astype(BF16)
    w1_b, w2_b = ffn_w_in.astype(BF16), ffn_w_out.astype(BF16)
    b_in3 = b_in.reshape(depth, 1, n_in)
    n1 = norm1_g.reshape(depth, 1, d)
    n2 = norm2_g.reshape(depth, 1, d)
    conv_b3 = conv_b.reshape(depth, 1, d_rnn)
    ba4 = lru_ba.reshape(depth, 2, 1, d_rnn)
    bx4 = lru_bx.reshape(depth, 2, 1, d_rnn)
    lam4 = lru_lambda.reshape(depth, 2, 1, d_rnn)
    h0_lat = jnp.broadcast_to(state_lru.transpose(1, 2, 0, 3)[:, :, :, None, :],
                              (depth, 2, n_lat, SUBLANES, d_rnn))
    h0_ctx = jnp.zeros((depth, 2, n_ctx, SUBLANES, d_rnn), F32)

    wc = jnp.asarray(_channel_dft(d_fnet), F32).astype(BF16)
    c_ctx_m, s_ctx_m = (jnp.asarray(m, F32).astype(BF16) for m in _seq_dft(l_ctx))
    c_lat_m, s_lat_m = (jnp.asarray(m, F32).astype(BF16) for m in _grid_dft(l_lat // GRID_W, GRID_W))

    as_lat = lambda a: a.reshape(-1, n_lat, SUBLANES, a.shape[-1])
    as_ctx = lambda a: a.reshape(-1, n_ctx, SUBLANES, a.shape[-1])

    states = []
    for l in range(depth):
        xr, gy, uv, gate = _inproj_call(l, x, mod, n1, w_in_b, b_in3, wc, tm_in, rows_lat // tm_in, d_rnn, d_fnet)

        rec, _, y_lat = _seq_mix_call(l, 0, lc_lat, 1, as_lat(xr), as_lat(gy), as_lat(uv), conv_w, conv_b3, wg,
                                      ba4, bx4, lam4, h0_lat, c_lat_m, s_lat_m, 64, "seq_mix_lat")
        rec, st, y_ctx = _seq_mix_call(l, ctx_block, lc_ctx, 4, as_ctx(rec), as_ctx(gy), as_ctx(uv), conv_w,
                                       conv_b3, wg, ba4, bx4, lam4, h0_ctx, c_ctx_m, s_ctx_m, 16, "seq_mix_ctx")
        states.append(jnp.stack([st[0, :, SUBLANES - 1], st[1, :, 0]], axis=1))

        x = _mix_ffn_call(l, x, rec.reshape(rows, d_rnn), y_lat.reshape(rows_lat, d_fnet),
                          y_ctx.reshape(rows_ctx, d_fnet), gate, mod, n2,
                          wl_b, wf_b, wo_b, w1_b, w2_b, tm_ffn, rows_lat // tm_ffn,
                          final_g.reshape(1, d) if l == depth - 1 else None)

    y_sample = _from_chunk_layout(x[0], n_lat, l_lat)
    y_prompt = _from_chunk_layout(x[1], n_ctx, l_ctx)
    new_state = jnp.stack(states, axis=1)
    return (y_prompt, y_sample, new_state)
```

```python
import functools

import numpy as np
import jax
import jax.numpy as jnp
from jax import lax
from jax.experimental import pallas as pl
from jax.experimental.pallas import tpu as pltpu

F32 = jnp.float32
BF16 = jnp.bfloat16

GRID_W = 64
N_LRU_HEADS = 4
LRU_C = 8.0
CONV_W = 4
CONV_LEFT = 2
N_FNET_GROUPS = 8
EPS = 1e-6
LOG2_E = 1.4426950408889634

SUBLANES = 8
MXU_DIM = 256
MOD_ROWS = 128
VMEM_LIMIT = 60 * 1024 * 1024


def _sigmoid(x):
    return 0.5 * jnp.tanh(0.5 * x) + 0.5


def _dot(a, b):
    return jnp.dot(a, b, preferred_element_type=F32)


def _params(sem):
    return pltpu.CompilerParams(dimension_semantics=sem, vmem_limit_bytes=VMEM_LIMIT)


def _resident(block_shape, index_map):
    return pl.BlockSpec(block_shape, index_map, pipeline_mode=pl.Buffered(1))


def _modulate(y, scale_ref, shift_ref):
    rows, d = y.shape
    y3 = y.reshape(rows // MOD_ROWS, MOD_ROWS, d)
    y3 = y3 * (1.0 + scale_ref[...])[None] + shift_ref[...][None]
    return y3.reshape(rows, d)


def _gated_add(x, gate_ref, upd):
    rows, d = x.shape
    x3 = x.reshape(rows // MOD_ROWS, MOD_ROWS, d)
    u3 = upd.reshape(rows // MOD_ROWS, MOD_ROWS, d)
    return (x3 + gate_ref[...][None] * u3).reshape(rows, d)


def _rmsnorm(x, g_ref):
    ms = jnp.mean(x * x, axis=-1, keepdims=True)
    return x * lax.rsqrt(ms + EPS) * g_ref[...]


def _ada_kernel(c_ref, w_ref, b_ref, o_ref):
    c = c_ref[...]
    cs = c * _sigmoid(c)
    o_ref[...] = _dot(cs.astype(BF16), w_ref[...].astype(BF16)) + b_ref[...]


def _ada_call(c_pat, ada_w, ada_b):
    depth, d, n6 = ada_w.shape
    rows = c_pat.shape[0]
    tn = 3072
    return pl.pallas_call(
        _ada_kernel,
        grid=(depth, n6 // tn),
        in_specs=[
            pl.BlockSpec((rows, d), lambda l, j: (0, 0)),
            pl.BlockSpec((None, d, tn), lambda l, j: (l, 0, j)),
            pl.BlockSpec((None, 1, tn), lambda l, j: (l, 0, j)),
        ],
        out_specs=pl.BlockSpec((None, rows, tn), lambda l, j: (l, 0, j)),
        out_shape=jax.ShapeDtypeStruct((depth, rows, n6), F32),
        compiler_params=_params(("arbitrary", "arbitrary")),
        name="ada_mod",
    )(c_pat, ada_w, ada_b.reshape(depth, 1, n6))


def _inproj_kernel(d_rnn, d_fnet, x_ref, sh_ref, sc_ref, g_ref, w_ref, b_ref, wc_ref,
                   xr_ref, gy_ref, uv_ref, gate_ref):
    h = _modulate(_rmsnorm(x_ref[...], g_ref), sc_ref, sh_ref).astype(BF16)
    o1, o2, o3 = d_rnn, 2 * d_rnn, 2 * d_rnn + d_fnet
    n_in = w_ref.shape[1]
    xr_ref[...] = _dot(h, w_ref[:, 0:o1]) + b_ref[:, 0:o1]
    gy_ref[...] = jax.nn.gelu(_dot(h, w_ref[:, o1:o2]) + b_ref[:, o1:o2]).astype(BF16)
    xf = _dot(h, w_ref[:, o2:o3]) + b_ref[:, o2:o3]
    uv_ref[...] = _dot(xf.astype(BF16), wc_ref[...])
    gate_ref[...] = _sigmoid(_dot(h, w_ref[:, o3:n_in]) + b_ref[:, o3:n_in]).astype(BF16)


def _inproj_call(l, x, mod, norm_g, w_in, b_in, wc, tm, lat_tiles, d_rnn, d_fnet):
    rows, d = x.shape
    depth, _, n_in = w_in.shape
    n_gate = n_in - 2 * d_rnn - d_fnet
    grp = lambda i: jnp.where(i < lat_tiles, 0, 1)
    return pl.pallas_call(
        functools.partial(_inproj_kernel, d_rnn, d_fnet),
        grid=(rows // tm,),
        in_specs=[
            pl.BlockSpec((tm, d), lambda i: (i, 0)),
            pl.BlockSpec((None, MOD_ROWS, d), lambda i: (l, grp(i), 0)),
            pl.BlockSpec((None, MOD_ROWS, d), lambda i: (l, grp(i), 1)),
            pl.BlockSpec((None, 1, d), lambda i: (l, 0, 0)),
            _resident((None, d, n_in), lambda i: (l, 0, 0)),
            pl.BlockSpec((None, 1, n_in), lambda i: (l, 0, 0)),
            _resident((d_fnet, 2 * d_fnet), lambda i: (0, 0)),
        ],
        out_specs=[
            pl.BlockSpec((tm, d_rnn), lambda i: (i, 0)),
            pl.BlockSpec((tm, d_rnn), lambda i: (i, 0)),
            pl.BlockSpec((tm, 2 * d_fnet), lambda i: (i, 0)),
            pl.BlockSpec((tm, n_gate), lambda i: (i, 0)),
        ],
        out_shape=[
            jax.ShapeDtypeStruct((rows, d_rnn), F32),
            jax.ShapeDtypeStruct((rows, d_rnn), BF16),
            jax.ShapeDtypeStruct((rows, 2 * d_fnet), F32),
            jax.ShapeDtypeStruct((rows, n_gate), BF16),
        ],
        compiler_params=_params(("arbitrary",)),
        name="in_proj",
    )(x, mod, mod, norm_g, w_in, b_in, wc)


def _seq_mix_kernel(ct, x_ref, cw_ref, cb_ref, wg_ref, ba_ref, bx_ref, lam_ref, h0_ref,
                    uv_ref, c_ref, s_ref, out_ref, st_ref, y_ref, a_s, b_s, h_s, p_s):
    lc, nb, _, hd = x_ref.shape
    tile = (nb, SUBLANES, hd)
    flat = nb * SUBLANES
    sub = lax.broadcasted_iota(jnp.int32, tile, 1)

    width = y_ref.shape[-1]
    seq_rows = lc * SUBLANES
    n_chunks = lc // ct
    piece = seq_rows // 2 // n_chunks
    side_by_side = lambda lo: jnp.concatenate(
        [uv_ref[:, s, :, lo:lo + width].reshape(seq_rows, width) for s in range(nb)], axis=1).astype(BF16)
    u_all, v_all = side_by_side(0), side_by_side(width)

    def fourier_piece(k):
        r0 = pl.multiple_of((pl.program_id(1) % 2) * (seq_rows // 2) + k * piece, piece)
        y = _dot(c_ref[pl.ds(r0, piece), :], u_all) - _dot(s_ref[pl.ds(r0, piece), :], v_all)
        tiles = slice(k * piece // SUBLANES, (k + 1) * piece // SUBLANES)
        for s in range(nb):
            y_ref[tiles, s] = y[:, s * width:(s + 1) * width].reshape(piece // SUBLANES, SUBLANES, width)

    def from_prev_chunk(v):
        n = v.shape[0] * flat
        r = pltpu.roll(v.reshape(n, hd), 1, 0).reshape(v.shape)
        return jnp.where(sub[None] == 0, 0.0, r)

    def from_next_chunk(v):
        n = v.shape[0] * flat
        r = pltpu.roll(v.reshape(n, hd), n - 1, 0).reshape(v.shape)
        return jnp.where(sub[None] == SUBLANES - 1, 0.0, r)

    def shifted(t0, off, n):
        lo, hi = t0 + off, t0 + off + n
        parts = []
        if lo < 0:
            parts.append(from_prev_chunk(x_ref[lc + lo:lc + min(hi, 0)]))
        if hi > 0 and lo < lc:
            parts.append(x_ref[max(lo, 0):min(hi, lc)])
        if hi > lc:
            parts.append(from_next_chunk(x_ref[max(lo, lc) - lc:hi - lc]))
        return parts[0] if len(parts) == 1 else jnp.concatenate(parts, axis=0)

    z = -lam_ref[...]
    softplus = jnp.maximum(z, 0.0) + jnp.log1p(jnp.exp(-jnp.abs(z)))
    half_l2 = (-0.5 * LRU_C * LOG2_E) * softplus
    half_ba = 0.5 * ba_ref[...]
    half_bx = 0.5 * bx_ref[...]

    for t0 in range(0, lc, ct):
        xc = cb_ref[...][None, None]
        for k in range(CONV_W):
            xc = xc + shifted(t0, k - CONV_LEFT, ct) * cw_ref[k:k + 1, :][None, None]
        xc2 = xc.reshape(ct * flat, hd)
        half_xc = 0.5 * xc2
        g = _dot(xc2.astype(BF16), wg_ref[...])
        fourier_piece(t0 // ct)
        for d in range(2):
            t_r = jnp.tanh(g[:, 2 * d * hd:(2 * d + 1) * hd] + half_ba[d])
            t_i = jnp.tanh(g[:, (2 * d + 1) * hd:(2 * d + 2) * hd] + half_bx[d])
            a = jnp.exp2(half_l2[d] * t_r + half_l2[d])
            m = 1.0 - a * a
            mult = jnp.where(m > 0.0, m * lax.rsqrt(m), 0.0)
            a_s[d, t0:t0 + ct] = a.reshape(ct, *tile)
            b_s[d, t0:t0 + ct] = ((mult * half_xc) * (t_i + 1.0)).reshape(ct, *tile)

    def scan_body(s, carry):
        hf, pf, hb, pb = carry
        tb = lc - 1 - s
        af, bf = a_s[0, s], b_s[0, s]
        ab, bb = a_s[1, tb], b_s[1, tb]
        hf = af * hf + bf
        pf = af * pf
        hb = ab * hb + bb
        pb = ab * pb
        h_s[0, s] = hf
        p_s[0, s] = pf
        h_s[1, tb] = hb
        p_s[1, tb] = pb
        return hf, pf, hb, pb

    zero, one = jnp.zeros(tile, F32), jnp.ones(tile, F32)
    hf, pf, hb, pb = lax.fori_loop(0, lc, scan_body, (zero, one, zero, one), unroll=8)

    def chain(h_end, p_end, h0, forward):
        shift, edge = (1, 0) if forward else (flat - 1, SUBLANES - 1)
        state = h0
        for _ in range(SUBLANES - 1):
            nxt = pltpu.roll((p_end * state + h_end).reshape(flat, hd), shift, 0).reshape(tile)
            state = jnp.where(sub == edge, h0, nxt)
        return state

    sf = chain(hf, pf, h0_ref[0], True)
    sb = chain(hb, pb, h0_ref[1], False)
    st_ref[0] = pf * sf + hf
    st_ref[1] = pb * sb + hb

    for t0 in range(0, lc, ct):
        sl = slice(t0, t0 + ct)
        out_ref[sl] = (h_s[0, sl] + p_s[0, sl] * sf[None]) + (h_s[1, sl] + p_s[1, sl] * sb[None])


def _seq_mix_call(l, lead_block, lc, nb, xr4, uv4, conv_w, conv_b, wg, ba, bx, lam, h0, cmat, smat, ct, name):
    n_tiles, n_seq, _, d_rnn = xr4.shape
    hd = d_rnn // N_LRU_HEADS
    assert N_LRU_HEADS == 2 * (uv4.shape[-1] // (2 * MXU_DIM))
    seq_len = lc * SUBLANES
    blk = (lc, nb, SUBLANES, hd)
    in_specs = [
        pl.BlockSpec(blk, lambda b, h: (lead_block, b, 0, h)),
        pl.BlockSpec((None, CONV_W, hd), lambda b, h: (l, 0, h)),
        pl.BlockSpec((None, 1, hd), lambda b, h: (l, 0, h)),
        pl.BlockSpec((None, None, hd, 4 * hd), lambda b, h: (l, h, 0, 0)),
        pl.BlockSpec((None, 2, 1, hd), lambda b, h: (l, 0, 0, h)),
        pl.BlockSpec((None, 2, 1, hd), lambda b, h: (l, 0, 0, h)),
        pl.BlockSpec((None, 2, 1, hd), lambda b, h: (l, 0, 0, h)),
        pl.BlockSpec((None, 2, nb, SUBLANES, hd), lambda b, h: (l, 0, b, 0, h)),
        pl.BlockSpec((lc, nb, SUBLANES, 2 * MXU_DIM), lambda b, h: (lead_block, b, 0, h // 2)),
        _resident((seq_len, seq_len), lambda b, h: (0, 0)),
        _resident((seq_len, seq_len), lambda b, h: (0, 0)),
    ]
    return pl.pallas_call(
        functools.partial(_seq_mix_kernel, ct),
        grid=(n_seq // nb, N_LRU_HEADS),
        in_specs=in_specs,
        out_specs=[
            pl.BlockSpec(blk, lambda b, h: (lead_block, b, 0, h)),
            pl.BlockSpec((2, nb, SUBLANES, hd), lambda b, h: (0, b, 0, h)),
            pl.BlockSpec((lc // 2, nb, SUBLANES, MXU_DIM), lambda b, h: (h % 2, b, 0, h // 2)),
        ],
        out_shape=[
            jax.ShapeDtypeStruct(xr4.shape, F32),
            jax.ShapeDtypeStruct((2, n_seq, SUBLANES, d_rnn), F32),
            jax.ShapeDtypeStruct((lc, n_seq, SUBLANES, uv4.shape[-1] // 2), F32),
        ],
        scratch_shapes=[
            pltpu.VMEM((2, lc, nb, SUBLANES, hd), F32),
            pltpu.VMEM((2, lc, nb, SUBLANES, hd), F32),
            pltpu.VMEM((2, lc, nb, SUBLANES, hd), F32),
            pltpu.VMEM((2, lc, nb, SUBLANES, hd), F32),
        ],
        input_output_aliases={0: 0},
        compiler_params=_params(("arbitrary", "arbitrary")),
        name=name,
    )(xr4, conv_w, conv_b, wg, ba, bx, lam, h0, uv4, cmat, smat)


def _mix_ffn_kernel(d_ff, lat_tiles, final, x_ref, rec_ref, gy_ref, ylat_ref, yctx_ref, gate_ref, g1_ref, sh2_ref,
                    sc2_ref, g2_ref, n2_ref, wl_ref, wf_ref, wo_ref, w1_ref, w2_ref, *rest):
    fin_ref = rest[0] if final else None
    outs, act_s = rest[1 if final else 0:-1], rest[-1]
    d = x_ref.shape[1]
    is_lat = pl.program_id(0) < lat_tiles
    out_a = _dot((rec_ref[...] * gy_ref[...]).astype(BF16), wl_ref[...])
    out_b = _dot(jnp.where(is_lat, ylat_ref[...], yctx_ref[...]).astype(BF16), wf_ref[...])
    merged = gate_ref[:, 0:d] * out_a + gate_ref[:, d:2 * d] * out_b
    x1 = _gated_add(x_ref[...], g1_ref, _dot(merged.astype(BF16), wo_ref[...]))
    h2 = _modulate(_rmsnorm(x1, n2_ref), sc2_ref, sh2_ref).astype(BF16)
    for c0, c1 in _ffn_chunks(d_ff):
        u = _dot(h2, w1_ref[:, c0:c1])
        v = _dot(h2, w1_ref[:, d_ff + c0:d_ff + c1])
        act_s[:, c0:c1] = ((u * _sigmoid(u)) * v).astype(BF16)
    x2 = _gated_add(x1, g2_ref, _dot(act_s[...], w2_ref[...]))
    if fin_ref is None:
        outs[0][...] = x2
    else:
        xn = _rmsnorm(x2, fin_ref)

        @pl.when(is_lat)
        def _():
            outs[0][...] = xn

        @pl.when(jnp.logical_not(is_lat))
        def _():
            outs[1][...] = xn


def _ffn_chunks(d_ff):
    n_tiles = d_ff // MXU_DIM
    half = (n_tiles + 1) // 2 * MXU_DIM
    return [(0, half), (half, d_ff)]


def _mix_ffn_call(l, x, rec, gy, y_lat, y_ctx, gate, mod, norm_g, wl, wf, wo, w1, w2, tm, lat_tiles, final_g):
    rows, d = x.shape
    d_rnn = rec.shape[1]
    d_fnet, d_ff = wf.shape[1], w2.shape[1]
    n_tiles = rows // tm
    grp = lambda i: jnp.where(i < lat_tiles, 0, 1)
    tile = lambda w: pl.BlockSpec((tm, w), lambda i: (i, 0))
    modspec = lambda j: pl.BlockSpec((None, MOD_ROWS, d), lambda i: (l, grp(i), j))
    lat_blk = lambda i: (jnp.minimum(i, lat_tiles - 1), 0)
    ctx_blk = lambda i: (jnp.maximum(i - lat_tiles, 0), 0)
    in_specs = [
        tile(d), tile(d_rnn), tile(d_rnn),
        pl.BlockSpec((tm, d_fnet), lat_blk), pl.BlockSpec((tm, d_fnet), ctx_blk), tile(2 * d),
        modspec(2), modspec(3), modspec(4), modspec(5),
        pl.BlockSpec((None, 1, d), lambda i: (l, 0, 0)),
        _resident((None, d_rnn, d), lambda i: (l, 0, 0)),
        _resident((None, d_fnet, d), lambda i: (l, 0, 0)),
        _resident((None, d, d), lambda i: (l, 0, 0)),
        _resident((None, d, 2 * d_ff), lambda i: (l, 0, 0)),
        _resident((None, d_ff, d), lambda i: (l, 0, 0)),
    ]
    args = [x, rec, gy, y_lat, y_ctx, gate, mod, mod, mod, mod, norm_g, wl, wf, wo, w1, w2]
    if final_g is None:
        out_specs = tile(d)
        out_shape = jax.ShapeDtypeStruct((rows, d), F32)
    else:
        in_specs.append(pl.BlockSpec((1, d), lambda i: (0, 0)))
        args.append(final_g)
        out_specs = [pl.BlockSpec((tm, d), lat_blk), pl.BlockSpec((tm, d), ctx_blk)]
        out_shape = [
            jax.ShapeDtypeStruct((lat_tiles * tm, d), F32),
            jax.ShapeDtypeStruct(((n_tiles - lat_tiles) * tm, d), F32),
        ]
    return pl.pallas_call(
        functools.partial(_mix_ffn_kernel, d_ff, lat_tiles, final_g is not None),
        grid=(n_tiles,),
        in_specs=in_specs,
        out_specs=out_specs,
        out_shape=out_shape,
        scratch_shapes=[pltpu.VMEM((tm, d_ff), BF16)],
        compiler_params=_params(("arbitrary",)),
        name="mix_ffn",
    )(*args)


def _cos_sin(n):
    k = np.arange(n)
    ang = 2.0 * np.pi * ((k[:, None] * k[None, :]) % n) / n
    return np.cos(ang), np.sin(ang)


def _channel_dft(d_fnet):
    gd = d_fnet // N_FNET_GROUPS
    c, s = _cos_sin(gd)
    eye = np.eye(N_FNET_GROUPS)
    scale = 1.0 / np.sqrt(gd)
    u, v = np.kron(eye, c) * scale, np.kron(eye, s) * scale
    blocks = []
    for j in range(0, d_fnet, MXU_DIM):
        blocks += [u[:, j:j + MXU_DIM], v[:, j:j + MXU_DIM]]
    return np.concatenate(blocks, axis=1)


def _row_order(seq_len):
    j = np.arange(seq_len)
    return (j % SUBLANES) * (seq_len // SUBLANES) + j // SUBLANES


def _seq_dft(n):
    c, s = _cos_sin(n)
    scale = 1.0 / np.sqrt(n)
    p = _row_order(n)
    return (c * scale)[p][:, p], (s * scale)[p][:, p]


def _grid_dft(n_rows, n_cols):
    cr, sr = _cos_sin(n_rows)
    cw, sw = _cos_sin(n_cols)
    scale = 1.0 / np.sqrt(n_rows * n_cols)
    p = _row_order(n_rows * n_cols)
    c = (np.kron(cr, cw) - np.kron(sr, sw)) * scale
    s = (np.kron(sr, cw) + np.kron(cr, sw)) * scale
    return c[p][:, p], s[p][:, p]


def _to_chunk_layout(x):
    b, l, d = x.shape
    return x.reshape(b, SUBLANES, l // SUBLANES, d).transpose(2, 0, 1, 3).reshape(b * l, d)


def _from_chunk_layout(rows, b, l):
    d = rows.shape[-1]
    return rows.reshape(l // SUBLANES, b, SUBLANES, d).transpose(1, 2, 0, 3).reshape(b, l, d)


def kernel(x_prompt, x_sample, state_lru, c, c_ctx, norm1_g, norm2_g, ada_w, ada_b, w_in, b_in, conv_w, conv_b,
           lru_wa, lru_ba, lru_wx, lru_bx, lru_lambda, w_lru_out, w_fnet_out, w_out, ffn_w_in, ffn_w_out,
           final_g):
    n_ctx, l_ctx, d = x_prompt.shape
    n_lat, l_lat, _ = x_sample.shape
    depth = w_in.shape[0]
    d_rnn = conv_w.shape[-1]
    d_fnet = w_fnet_out.shape[1]
    n_in = w_in.shape[-1]
    hd = d_rnn // N_LRU_HEADS
    assert MOD_ROWS % (n_lat * SUBLANES) == 0 and MOD_ROWS % (n_ctx * SUBLANES) == 0
    assert l_lat % GRID_W == 0 and n_in == 2 * d_rnn + d_fnet + 2 * d
    rows_lat, rows_ctx = n_lat * l_lat, n_ctx * l_ctx
    rows = rows_lat + rows_ctx
    assert rows_lat % rows_ctx == 0
    lc_lat, lc_ctx = l_lat // SUBLANES, l_ctx // SUBLANES
    ctx_block = rows_lat // rows_ctx

    tm_in, tm_ffn = 512, 512

    x = jnp.concatenate([_to_chunk_layout(x_sample), _to_chunk_layout(x_prompt)], axis=0)

    c_pat = jnp.concatenate([
        jnp.tile(jnp.repeat(c, SUBLANES, axis=0), (MOD_ROWS // (n_lat * SUBLANES), 1)),
        jnp.broadcast_to(c_ctx[None, :], (MOD_ROWS, d))], axis=0)
    mod = _ada_call(c_pat, ada_w, ada_b)

    w_in_b = w_in.astype(BF16)
    wg = (0.5 * jnp.concatenate([lru_wa[:, 0], lru_wx[:, 0], lru_wa[:, 1], lru_wx[:, 1]], axis=-1)).astype(BF16)
    wl_b, wf_b, wo_b = w_lru_out.astype(BF16), w_fnet_out.astype(BF16), w_out.astype(BF16)
    w1_b, w2_b = ffn_w_in.astype(BF16), ffn_w_out.astype(BF16)
    b_in3 = b_in.reshape(depth, 1, n_in)
    n1 = norm1_g.reshape(depth, 1, d)
    n2 = norm2_g.reshape(depth, 1, d)
    conv_b3 = conv_b.reshape(depth, 1, d_rnn)
    ba4 = lru_ba.reshape(depth, 2, 1, d_rnn)
    bx4 = lru_bx.reshape(depth, 2, 1, d_rnn)
    lam4 = lru_lambda.reshape(depth, 2, 1, d_rnn)
    h0_lat = jnp.broadcast_to(state_lru.transpose(1, 2, 0, 3)[:, :, :, None, :],
                              (depth, 2, n_lat, SUBLANES, d_rnn))
    h0_ctx = jnp.zeros((depth, 2, n_ctx, SUBLANES, d_rnn), F32)

    wc = jnp.asarray(_channel_dft(d_fnet), F32).astype(BF16)
    c_ctx_m, s_ctx_m = (jnp.asarray(m, F32).astype(BF16) for m in _seq_dft(l_ctx))
    c_lat_m, s_lat_m = (jnp.asarray(m, F32).astype(BF16) for m in _grid_dft(l_lat // GRID_W, GRID_W))

    as_lat = lambda a: a.reshape(-1, n_lat, SUBLANES, a.shape[-1])
    as_ctx = lambda a: a.reshape(-1, n_ctx, SUBLANES, a.shape[-1])

    states = []
    for l in range(depth):
        xr, gy, uv, gate = _inproj_call(l, x, mod, n1, w_in_b, b_in3, wc, tm_in, rows_lat // tm_in, d_rnn, d_fnet)

        rec, _, y_lat = _seq_mix_call(l, 0, lc_lat, 1, as_lat(xr), as_lat(uv), conv_w, conv_b3, wg,
                                      ba4, bx4, lam4, h0_lat, c_lat_m, s_lat_m, 64, "seq_mix_lat")
        rec, st, y_ctx = _seq_mix_call(l, ctx_block, lc_ctx, 4, as_ctx(rec), as_ctx(uv), conv_w,
                                       conv_b3, wg, ba4, bx4, lam4, h0_ctx, c_ctx_m, s_ctx_m, 16, "seq_mix_ctx")
        states.append(jnp.stack([st[0, :, SUBLANES - 1], st[1, :, 0]], axis=1))

        x = _mix_ffn_call(l, x, rec.reshape(rows, d_rnn), gy, y_lat.reshape(rows_lat, d_fnet),
                          y_ctx.reshape(rows_ctx, d_fnet), gate, mod, n2,
                          wl_b, wf_b, wo_b, w1_b, w2_b, tm_ffn, rows_lat // tm_ffn,
                          final_g.reshape(1, d) if l == depth - 1 else None)

    y_sample = _from_chunk_layout(x[0], n_lat, l_lat)
    y_prompt = _from_chunk_layout(x[1], n_ctx, l_ctx)
    new_state = jnp.stack(states, axis=1)
    return (y_prompt, y_sample, new_state)
```

```python
import functools

import numpy as np
import jax
import jax.numpy as jnp
from jax import lax
from jax.experimental import pallas as pl
from jax.experimental.pallas import tpu as pltpu

F32 = jnp.float32
BF16 = jnp.bfloat16

GRID_W = 64
N_LRU_HEADS = 4
LRU_C = 8.0
CONV_W = 4
CONV_LEFT = 2
N_FNET_GROUPS = 8
EPS = 1e-6
LOG2_E = 1.4426950408889634

SUBLANES = 8
MXU_DIM = 256
MOD_ROWS = 128
VMEM_LIMIT = 60 * 1024 * 1024


def _sigmoid(x):
    return 0.5 * jnp.tanh(0.5 * x) + 0.5


def _dot(a, b):
    return jnp.dot(a, b, preferred_element_type=F32)


def _params(sem):
    return pltpu.CompilerParams(dimension_semantics=sem, vmem_limit_bytes=VMEM_LIMIT)


def _resident(block_shape, index_map):
    return pl.BlockSpec(block_shape, index_map, pipeline_mode=pl.Buffered(1))


def _modulate(y, scale_ref, shift_ref):
    rows, d = y.shape
    y3 = y.reshape(rows // MOD_ROWS, MOD_ROWS, d)
    y3 = y3 * (1.0 + scale_ref[...])[None] + shift_ref[...][None]
    return y3.reshape(rows, d)


def _gated_add(x, gate_ref, upd):
    rows, d = x.shape
    x3 = x.reshape(rows // MOD_ROWS, MOD_ROWS, d)
    u3 = upd.reshape(rows // MOD_ROWS, MOD_ROWS, d)
    return (x3 + gate_ref[...][None] * u3).reshape(rows, d)


def _rmsnorm(x, g_ref):
    ms = jnp.mean(x * x, axis=-1, keepdims=True)
    return x * lax.rsqrt(ms + EPS) * g_ref[...]


def _ada_kernel(c_ref, w_ref, b_ref, o_ref):
    c = c_ref[...]
    cs = c * _sigmoid(c)
    o_ref[...] = _dot(cs.astype(BF16), w_ref[...].astype(BF16)) + b_ref[...]


def _ada_call(c_pat, ada_w, ada_b):
    depth, d, n6 = ada_w.shape
    rows = c_pat.shape[0]
    tn = 3072
    return pl.pallas_call(
        _ada_kernel,
        grid=(depth, n6 // tn),
        in_specs=[
            pl.BlockSpec((rows, d), lambda l, j: (0, 0)),
            pl.BlockSpec((None, d, tn), lambda l, j: (l, 0, j)),
            pl.BlockSpec((None, 1, tn), lambda l, j: (l, 0, j)),
        ],
        out_specs=pl.BlockSpec((None, rows, tn), lambda l, j: (l, 0, j)),
        out_shape=jax.ShapeDtypeStruct((depth, rows, n6), F32),
        compiler_params=_params(("arbitrary", "arbitrary")),
        name="ada_mod",
    )(c_pat, ada_w, ada_b.reshape(depth, 1, n6))


def _inproj_kernel(d_rnn, d_fnet, x_ref, sh_ref, sc_ref, g_ref, w_ref, b_ref, wc_ref,
                   xr_ref, gy_ref, uv_ref, gate_ref):
    h = _modulate(_rmsnorm(x_ref[...], g_ref), sc_ref, sh_ref).astype(BF16)
    o1, o2, o3 = d_rnn, 2 * d_rnn, 2 * d_rnn + d_fnet
    n_in = w_ref.shape[1]
    xr_ref[...] = _dot(h, w_ref[:, 0:o1]) + b_ref[:, 0:o1]
    gy_ref[...] = jax.nn.gelu(_dot(h, w_ref[:, o1:o2]) + b_ref[:, o1:o2]).astype(BF16)
    xf = _dot(h, w_ref[:, o2:o3]) + b_ref[:, o2:o3]
    uv_ref[...] = _dot(xf.astype(BF16), wc_ref[...])
    gate_ref[...] = _sigmoid(_dot(h, w_ref[:, o3:n_in]) + b_ref[:, o3:n_in]).astype(BF16)


def _inproj_call(l, x, mod, norm_g, w_in, b_in, wc, tm, lat_tiles, d_rnn, d_fnet):
    rows, d = x.shape
    n_in = w_in.shape[1]
    n_gate = n_in - 2 * d_rnn - d_fnet
    grp = lambda i: jnp.where(i < lat_tiles, 0, 1)
    return pl.pallas_call(
        functools.partial(_inproj_kernel, d_rnn, d_fnet),
        grid=(rows // tm,),
        in_specs=[
            pl.BlockSpec((tm, d), lambda i: (i, 0)),
            pl.BlockSpec((None, MOD_ROWS, d), lambda i: (l, grp(i), 0)),
            pl.BlockSpec((None, MOD_ROWS, d), lambda i: (l, grp(i), 1)),
            pl.BlockSpec((None, 1, d), lambda i: (l, 0, 0)),
            _resident((d, n_in), lambda i: (0, 0)),
            pl.BlockSpec((None, 1, n_in), lambda i: (l, 0, 0)),
            _resident((d_fnet, 2 * d_fnet), lambda i: (0, 0)),
        ],
        out_specs=[
            pl.BlockSpec((tm, d_rnn), lambda i: (i, 0)),
            pl.BlockSpec((tm, d_rnn), lambda i: (i, 0)),
            pl.BlockSpec((tm, 2 * d_fnet), lambda i: (i, 0)),
            pl.BlockSpec((tm, n_gate), lambda i: (i, 0)),
        ],
        out_shape=[
            jax.ShapeDtypeStruct((rows, d_rnn), F32),
            jax.ShapeDtypeStruct((rows, d_rnn), BF16),
            jax.ShapeDtypeStruct((rows, 2 * d_fnet), F32),
            jax.ShapeDtypeStruct((rows, n_gate), BF16),
        ],
        compiler_params=_params(("arbitrary",)),
        name="in_proj",
    )(x, mod, mod, norm_g, w_in, b_in, wc)


def _seq_mix_kernel(ct, n_cast, x_ref, cw_ref, cb_ref, wg_ref, ba_ref, bx_ref, lam_ref, h0_ref,
                    uv_ref, c_ref, s_ref, *rest):
    cast_in, (out_ref, st_ref, y_ref), cast_out = rest[:n_cast], rest[n_cast:n_cast + 3], rest[n_cast + 3:-4]
    a_s, b_s, h_s, p_s = rest[-4:]
    for w_ref, wb_ref in zip(cast_in, cast_out):
        wb_ref[...] = w_ref[...].astype(BF16)
    lc, nb, _, hd = x_ref.shape
    tile = (nb, SUBLANES, hd)
    flat = nb * SUBLANES
    sub = lax.broadcasted_iota(jnp.int32, tile, 1)

    width = y_ref.shape[-1]
    seq_rows = lc * SUBLANES
    n_chunks = lc // ct
    piece = seq_rows // 2 // n_chunks
    side_by_side = lambda lo: jnp.concatenate(
        [uv_ref[:, s, :, lo:lo + width].reshape(seq_rows, width) for s in range(nb)], axis=1).astype(BF16)
    u_all, v_all = side_by_side(0), side_by_side(width)

    def fourier_piece(k):
        r0 = pl.multiple_of((pl.program_id(1) % 2) * (seq_rows // 2) + k * piece, piece)
        y = _dot(c_ref[pl.ds(r0, piece), :], u_all) - _dot(s_ref[pl.ds(r0, piece), :], v_all)
        tiles = slice(k * piece // SUBLANES, (k + 1) * piece // SUBLANES)
        for s in range(nb):
            y_ref[tiles, s] = y[:, s * width:(s + 1) * width].reshape(piece // SUBLANES, SUBLANES, width)

    def from_prev_chunk(v):
        n = v.shape[0] * flat
        r = pltpu.roll(v.reshape(n, hd), 1, 0).reshape(v.shape)
        return jnp.where(sub[None] == 0, 0.0, r)

    def from_next_chunk(v):
        n = v.shape[0] * flat
        r = pltpu.roll(v.reshape(n, hd), n - 1, 0).reshape(v.shape)
        return jnp.where(sub[None] == SUBLANES - 1, 0.0, r)

    def shifted(t0, off, n):
        lo, hi = t0 + off, t0 + off + n
        parts = []
        if lo < 0:
            parts.append(from_prev_chunk(x_ref[lc + lo:lc + min(hi, 0)]))
        if hi > 0 and lo < lc:
            parts.append(x_ref[max(lo, 0):min(hi, lc)])
        if hi > lc:
            parts.append(from_next_chunk(x_ref[max(lo, lc) - lc:hi - lc]))
        return parts[0] if len(parts) == 1 else jnp.concatenate(parts, axis=0)

    z = -lam_ref[...]
    softplus = jnp.maximum(z, 0.0) + jnp.log1p(jnp.exp(-jnp.abs(z)))
    half_l2 = (-0.5 * LRU_C * LOG2_E) * softplus
    half_ba = 0.5 * ba_ref[...]
    half_bx = 0.5 * bx_ref[...]

    for t0 in range(0, lc, ct):
        xc = cb_ref[...][None, None]
        for k in range(CONV_W):
            xc = xc + shifted(t0, k - CONV_LEFT, ct) * cw_ref[k:k + 1, :][None, None]
        xc2 = xc.reshape(ct * flat, hd)
        half_xc = 0.5 * xc2
        g = _dot(xc2.astype(BF16), wg_ref[...])
        fourier_piece(t0 // ct)
        for d in range(2):
            t_r = jnp.tanh(g[:, 2 * d * hd:(2 * d + 1) * hd] + half_ba[d])
            t_i = jnp.tanh(g[:, (2 * d + 1) * hd:(2 * d + 2) * hd] + half_bx[d])
            a = jnp.exp2(half_l2[d] * t_r + half_l2[d])
            m = 1.0 - a * a
            mult = jnp.where(m > 0.0, m * lax.rsqrt(m), 0.0)
            a_s[d, t0:t0 + ct] = a.reshape(ct, *tile)
            b_s[d, t0:t0 + ct] = ((mult * half_xc) * (t_i + 1.0)).reshape(ct, *tile)

    def scan_body(s, carry):
        hf, pf, hb, pb = carry
        tb = lc - 1 - s
        af, bf = a_s[0, s], b_s[0, s]
        ab, bb = a_s[1, tb], b_s[1, tb]
        hf = af * hf + bf
        pf = af * pf
        hb = ab * hb + bb
        pb = ab * pb
        h_s[0, s] = hf
        p_s[0, s] = pf
        h_s[1, tb] = hb
        p_s[1, tb] = pb
        return hf, pf, hb, pb

    zero, one = jnp.zeros(tile, F32), jnp.ones(tile, F32)
    hf, pf, hb, pb = lax.fori_loop(0, lc, scan_body, (zero, one, zero, one), unroll=8)

    def chain(h_end, p_end, h0, forward):
        shift, edge = (1, 0) if forward else (flat - 1, SUBLANES - 1)
        state = h0
        for _ in range(SUBLANES - 1):
            nxt = pltpu.roll((p_end * state + h_end).reshape(flat, hd), shift, 0).reshape(tile)
            state = jnp.where(sub == edge, h0, nxt)
        return state

    sf = chain(hf, pf, h0_ref[0], True)
    sb = chain(hb, pb, h0_ref[1], False)
    st_ref[0] = pf * sf + hf
    st_ref[1] = pb * sb + hb

    for t0 in range(0, lc, ct):
        sl = slice(t0, t0 + ct)
        out_ref[sl] = (h_s[0, sl] + p_s[0, sl] * sf[None]) + (h_s[1, sl] + p_s[1, sl] * sb[None])


def _seq_mix_call(l, lead_block, lc, nb, xr4, uv4, conv_w, conv_b, wg, ba, bx, lam, h0, cmat, smat, ct, name,
                  casts=()):
    n_tiles, n_seq, _, d_rnn = xr4.shape
    hd = d_rnn // N_LRU_HEADS
    assert N_LRU_HEADS == 2 * (uv4.shape[-1] // (2 * MXU_DIM))
    seq_len = lc * SUBLANES
    blk = (lc, nb, SUBLANES, hd)
    n_steps = n_seq // nb * N_LRU_HEADS
    step = lambda b, h: b * N_LRU_HEADS + h
    cast_in_specs, cast_out_specs, cast_out_shapes = [], [], []
    for w, src_layer in casts:
        k, n = w.shape[1:]
        assert k % (n_steps * 2 * SUBLANES) == 0
        cast_in_specs.append(pl.BlockSpec((None, k // n_steps, n), lambda b, h, j=src_layer: (j, step(b, h), 0)))
        cast_out_specs.append(pl.BlockSpec((k // n_steps, n), lambda b, h: (step(b, h), 0)))
        cast_out_shapes.append(jax.ShapeDtypeStruct((k, n), BF16))
    in_specs = [
        pl.BlockSpec(blk, lambda b, h: (lead_block, b, 0, h)),
        pl.BlockSpec((None, CONV_W, hd), lambda b, h: (l, 0, h)),
        pl.BlockSpec((None, 1, hd), lambda b, h: (l, 0, h)),
        pl.BlockSpec((None, None, hd, 4 * hd), lambda b, h: (l, h, 0, 0)),
        pl.BlockSpec((None, 2, 1, hd), lambda b, h: (l, 0, 0, h)),
        pl.BlockSpec((None, 2, 1, hd), lambda b, h: (l, 0, 0, h)),
        pl.BlockSpec((None, 2, 1, hd), lambda b, h: (l, 0, 0, h)),
        pl.BlockSpec((None, 2, nb, SUBLANES, hd), lambda b, h: (l, 0, b, 0, h)),
        pl.BlockSpec((lc, nb, SUBLANES, 2 * MXU_DIM), lambda b, h: (lead_block, b, 0, h // 2)),
        _resident((seq_len, seq_len), lambda b, h: (0, 0)),
        _resident((seq_len, seq_len), lambda b, h: (0, 0)),
    ] + cast_in_specs
    return pl.pallas_call(
        functools.partial(_seq_mix_kernel, ct, len(casts)),
        grid=(n_seq // nb, N_LRU_HEADS),
        in_specs=in_specs,
        out_specs=[
            pl.BlockSpec(blk, lambda b, h: (lead_block, b, 0, h)),
            pl.BlockSpec((2, nb, SUBLANES, hd), lambda b, h: (0, b, 0, h)),
            pl.BlockSpec((lc // 2, nb, SUBLANES, MXU_DIM), lambda b, h: (h % 2, b, 0, h // 2)),
        ] + cast_out_specs,
        out_shape=[
            jax.ShapeDtypeStruct(xr4.shape, F32),
            jax.ShapeDtypeStruct((2, n_seq, SUBLANES, d_rnn), F32),
            jax.ShapeDtypeStruct((lc, n_seq, SUBLANES, uv4.shape[-1] // 2), F32),
        ] + cast_out_shapes,
        scratch_shapes=[
            pltpu.VMEM((2, lc, nb, SUBLANES, hd), F32),
            pltpu.VMEM((2, lc, nb, SUBLANES, hd), F32),
            pltpu.VMEM((2, lc, nb, SUBLANES, hd), F32),
            pltpu.VMEM((2, lc, nb, SUBLANES, hd), F32),
        ],
        input_output_aliases={0: 0},
        compiler_params=_params(("arbitrary", "arbitrary")),
        name=name,
    )(xr4, conv_w, conv_b, wg, ba, bx, lam, h0, uv4, cmat, smat, *[w for w, _ in casts])


def _mix_ffn_kernel(d_ff, lat_tiles, final, x_ref, rec_ref, gy_ref, ylat_ref, yctx_ref, gate_ref, g1_ref, sh2_ref,
                    sc2_ref, g2_ref, n2_ref, wl_ref, wf_ref, wo_ref, w1_ref, w2_ref, *rest):
    fin_ref = rest[0] if final else None
    outs, act_s = rest[1 if final else 0:-1], rest[-1]
    d = x_ref.shape[1]
    is_lat = pl.program_id(0) < lat_tiles
    out_a = _dot((rec_ref[...] * gy_ref[...]).astype(BF16), wl_ref[...])
    out_b = _dot(jnp.where(is_lat, ylat_ref[...], yctx_ref[...]).astype(BF16), wf_ref[...])
    merged = gate_ref[:, 0:d] * out_a + gate_ref[:, d:2 * d] * out_b
    x1 = _gated_add(x_ref[...], g1_ref, _dot(merged.astype(BF16), wo_ref[...]))
    h2 = _modulate(_rmsnorm(x1, n2_ref), sc2_ref, sh2_ref).astype(BF16)
    for c0, c1 in _ffn_chunks(d_ff):
        u = _dot(h2, w1_ref[:, c0:c1])
        v = _dot(h2, w1_ref[:, d_ff + c0:d_ff + c1])
        act_s[:, c0:c1] = ((u * _sigmoid(u)) * v).astype(BF16)
    x2 = _gated_add(x1, g2_ref, _dot(act_s[...], w2_ref[...]))
    if fin_ref is None:
        outs[0][...] = x2
    else:
        xn = _rmsnorm(x2, fin_ref)

        @pl.when(is_lat)
        def _():
            outs[0][...] = xn

        @pl.when(jnp.logical_not(is_lat))
        def _():
            outs[1][...] = xn


def _ffn_chunks(d_ff):
    n_tiles = d_ff // MXU_DIM
    half = (n_tiles + 1) // 2 * MXU_DIM
    return [(0, half), (half, d_ff)]


def _mix_ffn_call(l, x, rec, gy, y_lat, y_ctx, gate, mod, norm_g, wl, wf, wo, w1, w2, tm, lat_tiles, final_g):
    rows, d = x.shape
    d_rnn = rec.shape[1]
    d_fnet, d_ff = wf.shape[0], w2.shape[0]
    n_tiles = rows // tm
    grp = lambda i: jnp.where(i < lat_tiles, 0, 1)
    tile = lambda w: pl.BlockSpec((tm, w), lambda i: (i, 0))
    modspec = lambda j: pl.BlockSpec((None, MOD_ROWS, d), lambda i: (l, grp(i), j))
    lat_blk = lambda i: (jnp.minimum(i, lat_tiles - 1), 0)
    ctx_blk = lambda i: (jnp.maximum(i - lat_tiles, 0), 0)
    in_specs = [
        tile(d), tile(d_rnn), tile(d_rnn),
        pl.BlockSpec((tm, d_fnet), lat_blk), pl.BlockSpec((tm, d_fnet), ctx_blk), tile(2 * d),
        modspec(2), modspec(3), modspec(4), modspec(5),
        pl.BlockSpec((None, 1, d), lambda i: (l, 0, 0)),
    ] + [_resident(w.shape, lambda i: (0, 0)) for w in (wl, wf, wo, w1, w2)]
    args = [x, rec, gy, y_lat, y_ctx, gate, mod, mod, mod, mod, norm_g, wl, wf, wo, w1, w2]
    if final_g is None:
        out_specs = tile(d)
        out_shape = jax.ShapeDtypeStruct((rows, d), F32)
    else:
        in_specs.append(pl.BlockSpec((1, d), lambda i: (0, 0)))
        args.append(final_g)
        out_specs = [pl.BlockSpec((tm, d), lat_blk), pl.BlockSpec((tm, d), ctx_blk)]
        out_shape = [
            jax.ShapeDtypeStruct((lat_tiles * tm, d), F32),
            jax.ShapeDtypeStruct(((n_tiles - lat_tiles) * tm, d), F32),
        ]
    return pl.pallas_call(
        functools.partial(_mix_ffn_kernel, d_ff, lat_tiles, final_g is not None),
        grid=(n_tiles,),
        in_specs=in_specs,
        out_specs=out_specs,
        out_shape=out_shape,
        scratch_shapes=[pltpu.VMEM((tm, d_ff), BF16)],
        compiler_params=_params(("arbitrary",)),
        name="mix_ffn",
    )(*args)


def _cos_sin(n):
    k = np.arange(n)
    ang = 2.0 * np.pi * ((k[:, None] * k[None, :]) % n) / n
    return np.cos(ang), np.sin(ang)


def _channel_dft(d_fnet):
    gd = d_fnet // N_FNET_GROUPS
    c, s = _cos_sin(gd)
    eye = np.eye(N_FNET_GROUPS)
    scale = 1.0 / np.sqrt(gd)
    u, v = np.kron(eye, c) * scale, np.kron(eye, s) * scale
    blocks = []
    for j in range(0, d_fnet, MXU_DIM):
        blocks += [u[:, j:j + MXU_DIM], v[:, j:j + MXU_DIM]]
    return np.concatenate(blocks, axis=1)


def _row_order(seq_len):
    j = np.arange(seq_len)
    return (j % SUBLANES) * (seq_len // SUBLANES) + j // SUBLANES


def _seq_dft(n):
    c, s = _cos_sin(n)
    scale = 1.0 / np.sqrt(n)
    p = _row_order(n)
    return (c * scale)[p][:, p], (s * scale)[p][:, p]


def _grid_dft(n_rows, n_cols):
    cr, sr = _cos_sin(n_rows)
    cw, sw = _cos_sin(n_cols)
    scale = 1.0 / np.sqrt(n_rows * n_cols)
    p = _row_order(n_rows * n_cols)
    c = (np.kron(cr, cw) - np.kron(sr, sw)) * scale
    s = (np.kron(sr, cw) + np.kron(cr, sw)) * scale
    return c[p][:, p], s[p][:, p]


def _to_chunk_layout(x):
    b, l, d = x.shape
    return x.reshape(b, SUBLANES, l // SUBLANES, d).transpose(2, 0, 1, 3).reshape(b * l, d)


def _from_chunk_layout(rows, b, l):
    d = rows.shape[-1]
    return rows.reshape(l // SUBLANES, b, SUBLANES, d).transpose(1, 2, 0, 3).reshape(b, l, d)


def kernel(x_prompt, x_sample, state_lru, c, c_ctx, norm1_g, norm2_g, ada_w, ada_b, w_in, b_in, conv_w, conv_b,
           lru_wa, lru_ba, lru_wx, lru_bx, lru_lambda, w_lru_out, w_fnet_out, w_out, ffn_w_in, ffn_w_out,
           final_g):
    n_ctx, l_ctx, d = x_prompt.shape
    n_lat, l_lat, _ = x_sample.shape
    depth = w_in.shape[0]
    d_rnn = conv_w.shape[-1]
    d_fnet = w_fnet_out.shape[1]
    n_in = w_in.shape[-1]
    assert MOD_ROWS % (n_lat * SUBLANES) == 0 and MOD_ROWS % (n_ctx * SUBLANES) == 0
    assert l_lat % GRID_W == 0 and n_in == 2 * d_rnn + d_fnet + 2 * d
    rows_lat, rows_ctx = n_lat * l_lat, n_ctx * l_ctx
    rows = rows_lat + rows_ctx
    assert rows_lat % rows_ctx == 0
    lc_lat, lc_ctx = l_lat // SUBLANES, l_ctx // SUBLANES
    ctx_block = rows_lat // rows_ctx

    tm_in, tm_ffn = 1024, 512

    x = jnp.concatenate([_to_chunk_layout(x_sample), _to_chunk_layout(x_prompt)], axis=0)

    c_pat = jnp.concatenate([
        jnp.tile(jnp.repeat(c, SUBLANES, axis=0), (MOD_ROWS // (n_lat * SUBLANES), 1)),
        jnp.broadcast_to(c_ctx[None, :], (MOD_ROWS, d))], axis=0)
    mod = _ada_call(c_pat, ada_w, ada_b)

    w_in_b = w_in[0].astype(BF16)
    wg = (0.5 * jnp.concatenate([lru_wa[:, 0], lru_wx[:, 0], lru_wa[:, 1], lru_wx[:, 1]], axis=-1)).astype(BF16)
    b_in3 = b_in.reshape(depth, 1, n_in)
    n1 = norm1_g.reshape(depth, 1, d)
    n2 = norm2_g.reshape(depth, 1, d)
    conv_b3 = conv_b.reshape(depth, 1, d_rnn)
    ba4 = lru_ba.reshape(depth, 2, 1, d_rnn)
    bx4 = lru_bx.reshape(depth, 2, 1, d_rnn)
    lam4 = lru_lambda.reshape(depth, 2, 1, d_rnn)
    h0_lat = jnp.broadcast_to(state_lru.transpose(1, 2, 0, 3)[:, :, :, None, :],
                              (depth, 2, n_lat, SUBLANES, d_rnn))
    h0_ctx = jnp.zeros((depth, 2, n_ctx, SUBLANES, d_rnn), F32)

    wc = jnp.asarray(_channel_dft(d_fnet), F32).astype(BF16)
    c_ctx_m, s_ctx_m = (jnp.asarray(m, F32).astype(BF16) for m in _seq_dft(l_ctx))
    c_lat_m, s_lat_m = (jnp.asarray(m, F32).astype(BF16) for m in _grid_dft(l_lat // GRID_W, GRID_W))

    as_lat = lambda a: a.reshape(-1, n_lat, SUBLANES, a.shape[-1])
    as_ctx = lambda a: a.reshape(-1, n_ctx, SUBLANES, a.shape[-1])

    states = []
    for l in range(depth):
        xr, gy, uv, gate = _inproj_call(l, x, mod, n1, w_in_b, b_in3, wc, tm_in, rows_lat // tm_in, d_rnn, d_fnet)

        rec, _, y_lat = _seq_mix_call(l, 0, lc_lat, 1, as_lat(xr), as_lat(uv), conv_w, conv_b3, wg,
                                      ba4, bx4, lam4, h0_lat, c_lat_m, s_lat_m, 64, "seq_mix_lat")
        casts = [(w, l) for w in (w_lru_out, w_fnet_out, w_out, ffn_w_in, ffn_w_out)]
        if l + 1 < depth:
            casts.append((w_in, l + 1))
        rec, st, y_ctx, *w_b = _seq_mix_call(l, ctx_block, lc_ctx, 4, as_ctx(rec), as_ctx(uv), conv_w, conv_b3, wg,
                                             ba4, bx4, lam4, h0_ctx, c_ctx_m, s_ctx_m, 16, "seq_mix_ctx", casts)
        states.append(jnp.stack([st[0, :, SUBLANES - 1], st[1, :, 0]], axis=1))

        x = _mix_ffn_call(l, x, rec.reshape(rows, d_rnn), gy, y_lat.reshape(rows_lat, d_fnet),
                          y_ctx.reshape(rows_ctx, d_fnet), gate, mod, n2,
                          *w_b[:5], tm_ffn, rows_lat // tm_ffn,
                          final_g.reshape(1, d) if l == depth - 1 else None)
        if l + 1 < depth:
            w_in_b = w_b[5]

    y_sample = _from_chunk_layout(x[0], n_lat, l_lat)
    y_prompt = _from_chunk_layout(x[1], n_ctx, l_ctx)
    new_state = jnp.stack(states, axis=1)
    return (y_prompt, y_sample, new_state)
```

```python
import functools

import numpy as np
import jax
import jax.numpy as jnp
from jax import lax
from jax.experimental import pallas as pl
from jax.experimental.pallas import tpu as pltpu

F32 = jnp.float32
BF16 = jnp.bfloat16

GRID_W = 64
N_LRU_HEADS = 4
LRU_C = 8.0
CONV_W = 4
CONV_LEFT = 2
N_FNET_GROUPS = 8
EPS = 1e-6
LOG2_E = 1.4426950408889634

SUBLANES = 8
MXU_DIM = 256
MOD_ROWS = 128
VMEM_LIMIT = 60 * 1024 * 1024


def _sigmoid(x):
    return 0.5 * jnp.tanh(0.5 * x) + 0.5


def _dot(a, b):
    return jnp.dot(a, b, preferred_element_type=F32)


def _params(sem):
    return pltpu.CompilerParams(dimension_semantics=sem, vmem_limit_bytes=VMEM_LIMIT)


def _resident(block_shape, index_map):
    return pl.BlockSpec(block_shape, index_map, pipeline_mode=pl.Buffered(1))


def _modulate(y, scale_ref, shift_ref):
    rows, d = y.shape
    y3 = y.reshape(rows // MOD_ROWS, MOD_ROWS, d)
    y3 = y3 * (1.0 + scale_ref[...])[None] + shift_ref[...][None]
    return y3.reshape(rows, d)


def _gated_add(x, gate_ref, upd):
    rows, d = x.shape
    x3 = x.reshape(rows // MOD_ROWS, MOD_ROWS, d)
    u3 = upd.reshape(rows // MOD_ROWS, MOD_ROWS, d)
    return (x3 + gate_ref[...][None] * u3).reshape(rows, d)


def _rmsnorm(x, g_ref):
    ms = jnp.mean(x * x, axis=-1, keepdims=True)
    return x * lax.rsqrt(ms + EPS) * g_ref[...]


def _ada_kernel(c_ref, w_ref, b_ref, o_ref):
    c = c_ref[...]
    cs = c * _sigmoid(c)
    o_ref[...] = _dot(cs.astype(BF16), w_ref[...].astype(BF16)) + b_ref[...]


def _ada_call(c_pat, ada_w, ada_b):
    depth, d, n6 = ada_w.shape
    rows = c_pat.shape[0]
    tn = 3072
    return pl.pallas_call(
        _ada_kernel,
        grid=(depth, n6 // tn),
        in_specs=[
            pl.BlockSpec((rows, d), lambda l, j: (0, 0)),
            pl.BlockSpec((None, d, tn), lambda l, j: (l, 0, j)),
            pl.BlockSpec((None, 1, tn), lambda l, j: (l, 0, j)),
        ],
        out_specs=pl.BlockSpec((None, rows, tn), lambda l, j: (l, 0, j)),
        out_shape=jax.ShapeDtypeStruct((depth, rows, n6), F32),
        compiler_params=_params(("arbitrary", "arbitrary")),
        name="ada_mod",
    )(c_pat, ada_w, ada_b.reshape(depth, 1, n6))


def _group_tile(lat_tiles, lat_ref, ctx_ref):
    if ctx_ref is None:
        return lat_ref[...]
    return jnp.where(pl.program_id(0) < lat_tiles, lat_ref[...], ctx_ref[...])


def _group_specs(tm, width, lat_tiles):
    return [pl.BlockSpec((tm, width), lambda i: (jnp.minimum(i, lat_tiles - 1), 0)),
            pl.BlockSpec((tm, width), lambda i: (jnp.maximum(i - lat_tiles, 0), 0))]


def _inproj_kernel(d_rnn, d_fnet, lat_tiles, split_x, x_ref, *rest):
    xctx_ref = rest[0] if split_x else None
    sh_ref, sc_ref, g_ref, w_ref, b_ref, wc_ref, xr_ref, gy_ref, uv_ref, gate_ref = rest[1 if split_x else 0:]
    h = _modulate(_rmsnorm(_group_tile(lat_tiles, x_ref, xctx_ref), g_ref), sc_ref, sh_ref).astype(BF16)
    o1, o2, o3 = d_rnn, 2 * d_rnn, 2 * d_rnn + d_fnet
    n_in = w_ref.shape[1]
    xr_ref[...] = _dot(h, w_ref[:, 0:o1]) + b_ref[:, 0:o1]
    gy_ref[...] = jax.nn.gelu(_dot(h, w_ref[:, o1:o2]) + b_ref[:, o1:o2]).astype(BF16)
    xf = _dot(h, w_ref[:, o2:o3]) + b_ref[:, o2:o3]
    uv_ref[...] = _dot(xf.astype(BF16), wc_ref[...])
    gate_ref[...] = _sigmoid(_dot(h, w_ref[:, o3:n_in]) + b_ref[:, o3:n_in]).astype(BF16)


def _inproj_call(l, x, mod, norm_g, w_in, b_in, wc, tm, lat_tiles, d_rnn, d_fnet):
    xs = x if isinstance(x, tuple) else (x,)
    rows, d = sum(a.shape[0] for a in xs), xs[0].shape[1]
    n_in = w_in.shape[1]
    n_gate = n_in - 2 * d_rnn - d_fnet
    grp = lambda i: jnp.where(i < lat_tiles, 0, 1)
    x_specs = _group_specs(tm, d, lat_tiles) if len(xs) == 2 else [pl.BlockSpec((tm, d), lambda i: (i, 0))]
    return pl.pallas_call(
        functools.partial(_inproj_kernel, d_rnn, d_fnet, lat_tiles, len(xs) == 2),
        grid=(rows // tm,),
        in_specs=x_specs + [
            pl.BlockSpec((None, MOD_ROWS, d), lambda i: (l, grp(i), 0)),
            pl.BlockSpec((None, MOD_ROWS, d), lambda i: (l, grp(i), 1)),
            pl.BlockSpec((None, 1, d), lambda i: (l, 0, 0)),
            _resident((d, n_in), lambda i: (0, 0)),
            pl.BlockSpec((None, 1, n_in), lambda i: (l, 0, 0)),
            _resident((d_fnet, 2 * d_fnet), lambda i: (0, 0)),
        ],
        out_specs=[
            pl.BlockSpec((tm, d_rnn), lambda i: (i, 0)),
            pl.BlockSpec((tm, d_rnn), lambda i: (i, 0)),
            pl.BlockSpec((tm, 2 * d_fnet), lambda i: (i, 0)),
            pl.BlockSpec((tm, n_gate), lambda i: (i, 0)),
        ],
        out_shape=[
            jax.ShapeDtypeStruct((rows, d_rnn), F32),
            jax.ShapeDtypeStruct((rows, d_rnn), BF16),
            jax.ShapeDtypeStruct((rows, 2 * d_fnet), F32),
            jax.ShapeDtypeStruct((rows, n_gate), BF16),
        ],
        compiler_params=_params(("arbitrary",)),
        name="in_proj",
    )(*xs, mod, mod, norm_g, w_in, b_in, wc)


def _seq_mix_kernel(ct, n_cast, x_ref, cw_ref, cb_ref, wg_ref, ba_ref, bx_ref, lam_ref, h0_ref,
                    uv_ref, c_ref, s_ref, *rest):
    cast_in, (out_ref, st_ref, y_ref), cast_out = rest[:n_cast], rest[n_cast:n_cast + 3], rest[n_cast + 3:-4]
    a_s, b_s, h_s, p_s = rest[-4:]
    for w_ref, wb_ref in zip(cast_in, cast_out):
        wb_ref[...] = w_ref[...].astype(BF16)
    lc, nb, _, hd = x_ref.shape
    tile = (nb, SUBLANES, hd)
    flat = nb * SUBLANES
    sub = lax.broadcasted_iota(jnp.int32, tile, 1)

    width = y_ref.shape[-1]
    seq_rows = lc * SUBLANES
    n_chunks = lc // ct
    piece = seq_rows // 2 // n_chunks
    side_by_side = lambda lo: jnp.concatenate(
        [uv_ref[:, s, :, lo:lo + width].reshape(seq_rows, width) for s in range(nb)], axis=1).astype(BF16)
    u_all, v_all = side_by_side(0), side_by_side(width)

    def fourier_piece(k):
        r0 = pl.multiple_of((pl.program_id(1) % 2) * (seq_rows // 2) + k * piece, piece)
        y = _dot(c_ref[pl.ds(r0, piece), :], u_all) - _dot(s_ref[pl.ds(r0, piece), :], v_all)
        tiles = slice(k * piece // SUBLANES, (k + 1) * piece // SUBLANES)
        for s in range(nb):
            y_ref[tiles, s] = y[:, s * width:(s + 1) * width].reshape(piece // SUBLANES, SUBLANES, width)

    def from_prev_chunk(v):
        n = v.shape[0] * flat
        r = pltpu.roll(v.reshape(n, hd), 1, 0).reshape(v.shape)
        return jnp.where(sub[None] == 0, 0.0, r)

    def from_next_chunk(v):
        n = v.shape[0] * flat
        r = pltpu.roll(v.reshape(n, hd), n - 1, 0).reshape(v.shape)
        return jnp.where(sub[None] == SUBLANES - 1, 0.0, r)

    def shifted(t0, off, n):
        lo, hi = t0 + off, t0 + off + n
        parts = []
        if lo < 0:
            parts.append(from_prev_chunk(x_ref[lc + lo:lc + min(hi, 0)]))
        if hi > 0 and lo < lc:
            parts.append(x_ref[max(lo, 0):min(hi, lc)])
        if hi > lc:
            parts.append(from_next_chunk(x_ref[max(lo, lc) - lc:hi - lc]))
        return parts[0] if len(parts) == 1 else jnp.concatenate(parts, axis=0)

    z = -lam_ref[...]
    softplus = jnp.maximum(z, 0.0) + jnp.log1p(jnp.exp(-jnp.abs(z)))
    half_l2 = (-0.5 * LRU_C * LOG2_E) * softplus
    half_ba = 0.5 * ba_ref[...]
    half_bx = 0.5 * bx_ref[...]

    for t0 in range(0, lc, ct):
        xc = cb_ref[...][None, None]
        for k in range(CONV_W):
            xc = xc + shifted(t0, k - CONV_LEFT, ct) * cw_ref[k:k + 1, :][None, None]
        xc2 = xc.reshape(ct * flat, hd)
        half_xc = 0.5 * xc2
        g = _dot(xc2.astype(BF16), wg_ref[...])
        fourier_piece(t0 // ct)
        for d in range(2):
            t_r = jnp.tanh(g[:, 2 * d * hd:(2 * d + 1) * hd] + half_ba[d])
            t_i = jnp.tanh(g[:, (2 * d + 1) * hd:(2 * d + 2) * hd] + half_bx[d])
            a = jnp.exp2(half_l2[d] * t_r + half_l2[d])
            m = 1.0 - a * a
            mult = jnp.where(m > 0.0, m * lax.rsqrt(m), 0.0)
            a_s[d, t0:t0 + ct] = a.reshape(ct, *tile)
            b_s[d, t0:t0 + ct] = ((mult * half_xc) * (t_i + 1.0)).reshape(ct, *tile)

    def scan_body(s, carry):
        hf, pf, hb, pb = carry
        tb = lc - 1 - s
        af, bf = a_s[0, s], b_s[0, s]
        ab, bb = a_s[1, tb], b_s[1, tb]
        hf = af * hf + bf
        pf = af * pf
        hb = ab * hb + bb
        pb = ab * pb
        h_s[0, s] = hf
        p_s[0, s] = pf
        h_s[1, tb] = hb
        p_s[1, tb] = pb
        return hf, pf, hb, pb

    zero, one = jnp.zeros(tile, F32), jnp.ones(tile, F32)
    hf, pf, hb, pb = lax.fori_loop(0, lc, scan_body, (zero, one, zero, one), unroll=8)

    def chain(h_end, p_end, h0, forward):
        shift, edge = (1, 0) if forward else (flat - 1, SUBLANES - 1)
        state = h0
        for _ in range(SUBLANES - 1):
            nxt = pltpu.roll((p_end * state + h_end).reshape(flat, hd), shift, 0).reshape(tile)
            state = jnp.where(sub == edge, h0, nxt)
        return state

    sf = chain(hf, pf, h0_ref[0], True)
    sb = chain(hb, pb, h0_ref[1], False)
    st_ref[0] = pf * sf + hf
    st_ref[1] = pb * sb + hb

    for t0 in range(0, lc, ct):
        sl = slice(t0, t0 + ct)
        out_ref[sl] = (h_s[0, sl] + p_s[0, sl] * sf[None]) + (h_s[1, sl] + p_s[1, sl] * sb[None])


def _seq_mix_call(l, lead_block, lc, nb, xr4, uv4, conv_w, conv_b, wg, ba, bx, lam, h0, cmat, smat, ct, name,
                  casts=()):
    n_tiles, n_seq, _, d_rnn = xr4.shape
    hd = d_rnn // N_LRU_HEADS
    assert N_LRU_HEADS == 2 * (uv4.shape[-1] // (2 * MXU_DIM))
    seq_len = lc * SUBLANES
    blk = (lc, nb, SUBLANES, hd)
    n_steps = n_seq // nb * N_LRU_HEADS
    step = lambda b, h: b * N_LRU_HEADS + h
    cast_in_specs, cast_out_specs, cast_out_shapes = [], [], []
    for w, src_layer in casts:
        k, n = w.shape[1:]
        assert k % (n_steps * 2 * SUBLANES) == 0
        cast_in_specs.append(pl.BlockSpec((None, k // n_steps, n), lambda b, h, j=src_layer: (j, step(b, h), 0)))
        cast_out_specs.append(pl.BlockSpec((k // n_steps, n), lambda b, h: (step(b, h), 0)))
        cast_out_shapes.append(jax.ShapeDtypeStruct((k, n), BF16))
    in_specs = [
        pl.BlockSpec(blk, lambda b, h: (lead_block, b, 0, h)),
        pl.BlockSpec((None, CONV_W, hd), lambda b, h: (l, 0, h)),
        pl.BlockSpec((None, 1, hd), lambda b, h: (l, 0, h)),
        pl.BlockSpec((None, None, hd, 4 * hd), lambda b, h: (l, h, 0, 0)),
        pl.BlockSpec((None, 2, 1, hd), lambda b, h: (l, 0, 0, h)),
        pl.BlockSpec((None, 2, 1, hd), lambda b, h: (l, 0, 0, h)),
        pl.BlockSpec((None, 2, 1, hd), lambda b, h: (l, 0, 0, h)),
        pl.BlockSpec((None, 2, nb, SUBLANES, hd), lambda b, h: (l, 0, b, 0, h)),
        pl.BlockSpec((lc, nb, SUBLANES, 2 * MXU_DIM), lambda b, h: (lead_block, b, 0, h // 2)),
        _resident((seq_len, seq_len), lambda b, h: (0, 0)),
        _resident((seq_len, seq_len), lambda b, h: (0, 0)),
    ] + cast_in_specs
    return pl.pallas_call(
        functools.partial(_seq_mix_kernel, ct, len(casts)),
        grid=(n_seq // nb, N_LRU_HEADS),
        in_specs=in_specs,
        out_specs=[
            pl.BlockSpec(blk, lambda b, h: (lead_block, b, 0, h)),
            pl.BlockSpec((2, nb, SUBLANES, hd), lambda b, h: (0, b, 0, h)),
            pl.BlockSpec((lc // 2, nb, SUBLANES, MXU_DIM), lambda b, h: (h % 2, b, 0, h // 2)),
        ] + cast_out_specs,
        out_shape=[
            jax.ShapeDtypeStruct(xr4.shape, F32),
            jax.ShapeDtypeStruct((2, n_seq, SUBLANES, d_rnn), F32),
            jax.ShapeDtypeStruct((lc, n_seq, SUBLANES, uv4.shape[-1] // 2), F32),
        ] + cast_out_shapes,
        scratch_shapes=[
            pltpu.VMEM((2, lc, nb, SUBLANES, hd), F32),
            pltpu.VMEM((2, lc, nb, SUBLANES, hd), F32),
            pltpu.VMEM((2, lc, nb, SUBLANES, hd), F32),
            pltpu.VMEM((2, lc, nb, SUBLANES, hd), F32),
        ],
        input_output_aliases={0: 0},
        compiler_params=_params(("arbitrary", "arbitrary")),
        name=name,
    )(xr4, conv_w, conv_b, wg, ba, bx, lam, h0, uv4, cmat, smat, *[w for w, _ in casts])


def _mix_ffn_kernel(d_ff, lat_tiles, final, split_x, x_ref, *rest):
    xctx_ref = rest[0] if split_x else None
    (rec_ref, gy_ref, ylat_ref, yctx_ref, gate_ref, g1_ref, sh2_ref, sc2_ref, g2_ref, n2_ref,
     wl_ref, wf_ref, wo_ref, w1_ref, w2_ref) = rest[1 if split_x else 0:][:15]
    rest = rest[(1 if split_x else 0) + 15:]
    fin_ref = rest[0] if final else None
    outs, act_s = rest[1 if final else 0:-1], rest[-1]
    d = x_ref.shape[1]
    is_lat = pl.program_id(0) < lat_tiles
    out_a = _dot((rec_ref[...] * gy_ref[...]).astype(BF16), wl_ref[...])
    out_b = _dot(_group_tile(lat_tiles, ylat_ref, yctx_ref).astype(BF16), wf_ref[...])
    merged = gate_ref[:, 0:d] * out_a + gate_ref[:, d:2 * d] * out_b
    x1 = _gated_add(_group_tile(lat_tiles, x_ref, xctx_ref), g1_ref, _dot(merged.astype(BF16), wo_ref[...]))
    h2 = _modulate(_rmsnorm(x1, n2_ref), sc2_ref, sh2_ref).astype(BF16)
    for c0, c1 in _ffn_chunks(d_ff):
        u = _dot(h2, w1_ref[:, c0:c1])
        v = _dot(h2, w1_ref[:, d_ff + c0:d_ff + c1])
        act_s[:, c0:c1] = ((u * _sigmoid(u)) * v).astype(BF16)
    x2 = _gated_add(x1, g2_ref, _dot(act_s[...], w2_ref[...]))
    if fin_ref is None:
        outs[0][...] = x2
    else:
        xn = _rmsnorm(x2, fin_ref)

        @pl.when(is_lat)
        def _():
            outs[0][...] = xn

        @pl.when(jnp.logical_not(is_lat))
        def _():
            outs[1][...] = xn


def _ffn_chunks(d_ff):
    n_tiles = d_ff // MXU_DIM
    half = (n_tiles + 1) // 2 * MXU_DIM
    return [(0, half), (half, d_ff)]


def _mix_ffn_call(l, x, rec, gy, y_lat, y_ctx, gate, mod, norm_g, wl, wf, wo, w1, w2, tm, lat_tiles, final_g):
    xs = x if isinstance(x, tuple) else (x,)
    rows, d = sum(a.shape[0] for a in xs), xs[0].shape[1]
    d_rnn = rec.shape[1]
    d_fnet, d_ff = wf.shape[0], w2.shape[0]
    n_tiles = rows // tm
    grp = lambda i: jnp.where(i < lat_tiles, 0, 1)
    tile = lambda w: pl.BlockSpec((tm, w), lambda i: (i, 0))
    modspec = lambda j: pl.BlockSpec((None, MOD_ROWS, d), lambda i: (l, grp(i), j))
    in_specs = (_group_specs(tm, d, lat_tiles) if len(xs) == 2 else [tile(d)]) + [
        tile(d_rnn), tile(d_rnn), *_group_specs(tm, d_fnet, lat_tiles), tile(2 * d),
        modspec(2), modspec(3), modspec(4), modspec(5),
        pl.BlockSpec((None, 1, d), lambda i: (l, 0, 0)),
    ] + [_resident(w.shape, lambda i: (0, 0)) for w in (wl, wf, wo, w1, w2)]
    args = [*xs, rec, gy, y_lat, y_ctx, gate, mod, mod, mod, mod, norm_g, wl, wf, wo, w1, w2]
    if final_g is None:
        out_specs = tile(d)
        out_shape = jax.ShapeDtypeStruct((rows, d), F32)
    else:
        in_specs.append(pl.BlockSpec((1, d), lambda i: (0, 0)))
        args.append(final_g)
        out_specs = _group_specs(tm, d, lat_tiles)
        out_shape = [
            jax.ShapeDtypeStruct((lat_tiles * tm, d), F32),
            jax.ShapeDtypeStruct(((n_tiles - lat_tiles) * tm, d), F32),
        ]
    return pl.pallas_call(
        functools.partial(_mix_ffn_kernel, d_ff, lat_tiles, final_g is not None, len(xs) == 2),
        grid=(n_tiles,),
        in_specs=in_specs,
        out_specs=out_specs,
        out_shape=out_shape,
        scratch_shapes=[pltpu.VMEM((tm, d_ff), BF16)],
        compiler_params=_params(("arbitrary",)),
        name="mix_ffn",
    )(*args)


def _cos_sin(n):
    k = np.arange(n)
    ang = 2.0 * np.pi * ((k[:, None] * k[None, :]) % n) / n
    return np.cos(ang), np.sin(ang)


def _channel_dft(d_fnet):
    gd = d_fnet // N_FNET_GROUPS
    c, s = _cos_sin(gd)
    eye = np.eye(N_FNET_GROUPS)
    scale = 1.0 / np.sqrt(gd)
    u, v = np.kron(eye, c) * scale, np.kron(eye, s) * scale
    blocks = []
    for j in range(0, d_fnet, MXU_DIM):
        blocks += [u[:, j:j + MXU_DIM], v[:, j:j + MXU_DIM]]
    return np.concatenate(blocks, axis=1)


def _row_order(seq_len):
    j = np.arange(seq_len)
    return (j % SUBLANES) * (seq_len // SUBLANES) + j // SUBLANES


def _seq_dft(n):
    c, s = _cos_sin(n)
    scale = 1.0 / np.sqrt(n)
    p = _row_order(n)
    return (c * scale)[p][:, p], (s * scale)[p][:, p]


def _grid_dft(n_rows, n_cols):
    cr, sr = _cos_sin(n_rows)
    cw, sw = _cos_sin(n_cols)
    scale = 1.0 / np.sqrt(n_rows * n_cols)
    p = _row_order(n_rows * n_cols)
    c = (np.kron(cr, cw) - np.kron(sr, sw)) * scale
    s = (np.kron(sr, cw) + np.kron(cr, sw)) * scale
    return c[p][:, p], s[p][:, p]


def _to_chunk_layout(x):
    b, l, d = x.shape
    return x.reshape(b, SUBLANES, l // SUBLANES, d).transpose(2, 0, 1, 3).reshape(b * l, d)


def _from_chunk_layout(rows, b, l):
    d = rows.shape[-1]
    return rows.reshape(l // SUBLANES, b, SUBLANES, d).transpose(1, 2, 0, 3).reshape(b, l, d)


def kernel(x_prompt, x_sample, state_lru, c, c_ctx, norm1_g, norm2_g, ada_w, ada_b, w_in, b_in, conv_w, conv_b,
           lru_wa, lru_ba, lru_wx, lru_bx, lru_lambda, w_lru_out, w_fnet_out, w_out, ffn_w_in, ffn_w_out,
           final_g):
    n_ctx, l_ctx, d = x_prompt.shape
    n_lat, l_lat, _ = x_sample.shape
    depth = w_in.shape[0]
    d_rnn = conv_w.shape[-1]
    d_fnet = w_fnet_out.shape[1]
    n_in = w_in.shape[-1]
    assert MOD_ROWS % (n_lat * SUBLANES) == 0 and MOD_ROWS % (n_ctx * SUBLANES) == 0
    assert l_lat % GRID_W == 0 and n_in == 2 * d_rnn + d_fnet + 2 * d
    rows_lat, rows_ctx = n_lat * l_lat, n_ctx * l_ctx
    rows = rows_lat + rows_ctx
    assert rows_lat % rows_ctx == 0
    lc_lat, lc_ctx = l_lat // SUBLANES, l_ctx // SUBLANES
    ctx_block = rows_lat // rows_ctx

    tm_in, tm_ffn = 1024, 512
    tm_in_first = 512

    x = (_to_chunk_layout(x_sample), _to_chunk_layout(x_prompt))

    c_pat = jnp.concatenate([
        jnp.tile(jnp.repeat(c, SUBLANES, axis=0), (MOD_ROWS // (n_lat * SUBLANES), 1)),
        jnp.broadcast_to(c_ctx[None, :], (MOD_ROWS, d))], axis=0)
    mod = _ada_call(c_pat, ada_w, ada_b)

    w_in_b = w_in[0].astype(BF16)
    wg = (0.5 * jnp.concatenate([lru_wa[:, 0], lru_wx[:, 0], lru_wa[:, 1], lru_wx[:, 1]], axis=-1)).astype(BF16)
    b_in3 = b_in.reshape(depth, 1, n_in)
    n1 = norm1_g.reshape(depth, 1, d)
    n2 = norm2_g.reshape(depth, 1, d)
    conv_b3 = conv_b.reshape(depth, 1, d_rnn)
    ba4 = lru_ba.reshape(depth, 2, 1, d_rnn)
    bx4 = lru_bx.reshape(depth, 2, 1, d_rnn)
    lam4 = lru_lambda.reshape(depth, 2, 1, d_rnn)
    h0_lat = jnp.broadcast_to(state_lru.transpose(1, 2, 0, 3)[:, :, :, None, :],
                              (depth, 2, n_lat, SUBLANES, d_rnn))
    h0_ctx = jnp.zeros((depth, 2, n_ctx, SUBLANES, d_rnn), F32)

    wc = jnp.asarray(_channel_dft(d_fnet), F32).astype(BF16)
    c_ctx_m, s_ctx_m = (jnp.asarray(m, F32).astype(BF16) for m in _seq_dft(l_ctx))
    c_lat_m, s_lat_m = (jnp.asarray(m, F32).astype(BF16) for m in _grid_dft(l_lat // GRID_W, GRID_W))

    as_lat = lambda a: a.reshape(-1, n_lat, SUBLANES, a.shape[-1])
    as_ctx = lambda a: a.reshape(-1, n_ctx, SUBLANES, a.shape[-1])

    states = []
    for l in range(depth):
        tm = tm_in_first if l == 0 else tm_in
        xr, gy, uv, gate = _inproj_call(l, x, mod, n1, w_in_b, b_in3, wc, tm, rows_lat // tm, d_rnn, d_fnet)

        rec, _, y_lat, w1_b = _seq_mix_call(l, 0, lc_lat, 1, as_lat(xr), as_lat(uv), conv_w, conv_b3, wg,
                                            ba4, bx4, lam4, h0_lat, c_lat_m, s_lat_m, 64, "seq_mix_lat",
                                            [(ffn_w_in, l)])
        casts = [(w, l) for w in (w_lru_out, w_fnet_out, w_out, ffn_w_out)]
        if l + 1 < depth:
            casts.append((w_in, l + 1))
        rec, st, y_ctx, *w_b = _seq_mix_call(l, ctx_block, lc_ctx, 4, as_ctx(rec), as_ctx(uv), conv_w, conv_b3, wg,
                                             ba4, bx4, lam4, h0_ctx, c_ctx_m, s_ctx_m, 16, "seq_mix_ctx", casts)
        states.append(jnp.stack([st[0, :, SUBLANES - 1], st[1, :, 0]], axis=1))

        x = _mix_ffn_call(l, x, rec.reshape(rows, d_rnn), gy, y_lat.reshape(rows_lat, d_fnet),
                          y_ctx.reshape(rows_ctx, d_fnet), gate, mod, n2,
                          w_b[0], w_b[1], w_b[2], w1_b, w_b[3], tm_ffn, rows_lat // tm_ffn,
                          final_g.reshape(1, d) if l == depth - 1 else None)
        if l + 1 < depth:
            w_in_b = w_b[4]

    y_sample = _from_chunk_layout(x[0], n_lat, l_lat)
    y_prompt = _from_chunk_layout(x[1], n_ctx, l_ctx)
    new_state = jnp.stack(states, axis=1)
    return (y_prompt, y_sample, new_state)
```

```python
import functools

import numpy as np
import jax
import jax.numpy as jnp
from jax import lax
from jax.experimental import pallas as pl
from jax.experimental.pallas import tpu as pltpu

F32 = jnp.float32
BF16 = jnp.bfloat16

GRID_W = 64
N_LRU_HEADS = 4
LRU_C = 8.0
CONV_W = 4
CONV_LEFT = 2
N_FNET_GROUPS = 8
EPS = 1e-6
LOG2_E = 1.4426950408889634

SUBLANES = 8
MXU_DIM = 256
MOD_ROWS = 128
VMEM_LIMIT = 60 * 1024 * 1024


def _sigmoid(x):
    return 0.5 * jnp.tanh(0.5 * x) + 0.5


def _dot(a, b):
    return jnp.dot(a, b, preferred_element_type=F32)


def _params(sem):
    return pltpu.CompilerParams(dimension_semantics=sem, vmem_limit_bytes=VMEM_LIMIT)


def _resident(block_shape, index_map):
    return pl.BlockSpec(block_shape, index_map, pipeline_mode=pl.Buffered(1))


def _modulate(y, scale_ref, shift_ref):
    rows, d = y.shape
    y3 = y.reshape(rows // MOD_ROWS, MOD_ROWS, d)
    y3 = y3 * (1.0 + scale_ref[...])[None] + shift_ref[...][None]
    return y3.reshape(rows, d)


def _gated_add(x, gate_ref, upd):
    rows, d = x.shape
    x3 = x.reshape(rows // MOD_ROWS, MOD_ROWS, d)
    u3 = upd.reshape(rows // MOD_ROWS, MOD_ROWS, d)
    return (x3 + gate_ref[...][None] * u3).reshape(rows, d)


def _rmsnorm(x, g_ref):
    ms = jnp.mean(x * x, axis=-1, keepdims=True)
    return x * lax.rsqrt(ms + EPS) * g_ref[...]


def _ada_kernel(c_ref, w_ref, b_ref, o_ref):
    @pl.when(pl.program_id(1) == 0)
    def _():
        o_ref[...] = jnp.broadcast_to(b_ref[...], o_ref.shape)

    c = c_ref[...]
    cs = c * _sigmoid(c)
    o_ref[...] += _dot(cs.astype(BF16), w_ref[...].astype(BF16))


def _ada_call(c_pat, ada_w, ada_b):
    depth, d, n6 = ada_w.shape
    rows = c_pat.shape[0]
    tk = MXU_DIM
    return pl.pallas_call(
        _ada_kernel,
        grid=(depth, d // tk),
        in_specs=[
            pl.BlockSpec((rows, tk), lambda l, k: (0, k)),
            pl.BlockSpec((None, tk, n6), lambda l, k: (l, k, 0)),
            pl.BlockSpec((None, 1, n6), lambda l, k: (l, 0, 0)),
        ],
        out_specs=pl.BlockSpec((None, rows, n6), lambda l, k: (l, 0, 0)),
        out_shape=jax.ShapeDtypeStruct((depth, rows, n6), F32),
        compiler_params=_params(("arbitrary", "arbitrary")),
        name="ada_mod",
    )(c_pat, ada_w, ada_b.reshape(depth, 1, n6))


def _group_tile(lat_tiles, lat_ref, ctx_ref):
    if ctx_ref is None:
        return lat_ref[...]
    return jnp.where(pl.program_id(0) < lat_tiles, lat_ref[...], ctx_ref[...])


def _group_specs(tm, width, lat_tiles):
    return [pl.BlockSpec((tm, width), lambda i: (jnp.minimum(i, lat_tiles - 1), 0)),
            pl.BlockSpec((tm, width), lambda i: (jnp.maximum(i - lat_tiles, 0), 0))]


def _inproj_kernel(d_rnn, d_fnet, lat_tiles, split_x, x_ref, *rest):
    xctx_ref = rest[0] if split_x else None
    sh_ref, sc_ref, g_ref, w_ref, b_ref, wc_ref, xr_ref, gy_ref, uv_ref, gate_ref = rest[1 if split_x else 0:]
    h = _modulate(_rmsnorm(_group_tile(lat_tiles, x_ref, xctx_ref), g_ref), sc_ref, sh_ref).astype(BF16)
    o1, o2, o3 = d_rnn, 2 * d_rnn, 2 * d_rnn + d_fnet
    n_in = w_ref.shape[1]
    xr_ref[...] = _dot(h, w_ref[:, 0:o1]) + b_ref[:, 0:o1]
    gy_ref[...] = jax.nn.gelu(_dot(h, w_ref[:, o1:o2]) + b_ref[:, o1:o2]).astype(BF16)
    xf = _dot(h, w_ref[:, o2:o3]) + b_ref[:, o2:o3]
    uv_ref[...] = _dot(xf.astype(BF16), wc_ref[...])
    gate_ref[...] = _sigmoid(_dot(h, w_ref[:, o3:n_in]) + b_ref[:, o3:n_in]).astype(BF16)


def _inproj_call(l, x, mod, norm_g, w_in, b_in, wc, tm, lat_tiles, d_rnn, d_fnet):
    xs = x if isinstance(x, tuple) else (x,)
    rows, d = sum(a.shape[0] for a in xs), xs[0].shape[1]
    n_in = w_in.shape[1]
    n_gate = n_in - 2 * d_rnn - d_fnet
    grp = lambda i: jnp.where(i < lat_tiles, 0, 1)
    x_specs = _group_specs(tm, d, lat_tiles) if len(xs) == 2 else [pl.BlockSpec((tm, d), lambda i: (i, 0))]
    return pl.pallas_call(
        functools.partial(_inproj_kernel, d_rnn, d_fnet, lat_tiles, len(xs) == 2),
        grid=(rows // tm,),
        in_specs=x_specs + [
            pl.BlockSpec((None, MOD_ROWS, d), lambda i: (l, grp(i), 0)),
            pl.BlockSpec((None, MOD_ROWS, d), lambda i: (l, grp(i), 1)),
            pl.BlockSpec((None, 1, d), lambda i: (l, 0, 0)),
            _resident((d, n_in), lambda i: (0, 0)),
            pl.BlockSpec((None, 1, n_in), lambda i: (l, 0, 0)),
            _resident((d_fnet, 2 * d_fnet), lambda i: (0, 0)),
        ],
        out_specs=[
            pl.BlockSpec((tm, d_rnn), lambda i: (i, 0)),
            pl.BlockSpec((tm, d_rnn), lambda i: (i, 0)),
            pl.BlockSpec((tm, 2 * d_fnet), lambda i: (i, 0)),
            pl.BlockSpec((tm, n_gate), lambda i: (i, 0)),
        ],
        out_shape=[
            jax.ShapeDtypeStruct((rows, d_rnn), F32),
            jax.ShapeDtypeStruct((rows, d_rnn), BF16),
            jax.ShapeDtypeStruct((rows, 2 * d_fnet), F32),
            jax.ShapeDtypeStruct((rows, n_gate), BF16),
        ],
        compiler_params=_params(("arbitrary",)),
        name="in_proj",
    )(*xs, mod, mod, norm_g, w_in, b_in, wc)


def _seq_mix_kernel(ct, n_cast, x_ref, cw_ref, cb_ref, wg_ref, ba_ref, bx_ref, lam_ref, h0_ref,
                    uv_ref, c_ref, s_ref, *rest):
    cast_in, (out_ref, st_ref, y_ref), cast_out = rest[:n_cast], rest[n_cast:n_cast + 3], rest[n_cast + 3:-4]
    a_s, b_s, h_s, p_s = rest[-4:]
    for w_ref, wb_ref in zip(cast_in, cast_out):
        wb_ref[...] = w_ref[...].astype(BF16)
    lc, nb, _, hd = x_ref.shape
    tile = (nb, SUBLANES, hd)
    flat = nb * SUBLANES
    sub = lax.broadcasted_iota(jnp.int32, tile, 1)

    width = y_ref.shape[-1]
    seq_rows = lc * SUBLANES
    n_chunks = lc // ct
    piece = seq_rows // 2 // n_chunks
    side_by_side = lambda lo: jnp.concatenate(
        [uv_ref[:, s, :, lo:lo + width].reshape(seq_rows, width) for s in range(nb)], axis=1).astype(BF16)
    u_all, v_all = side_by_side(0), side_by_side(width)

    def fourier_piece(k):
        r0 = pl.multiple_of((pl.program_id(1) % 2) * (seq_rows // 2) + k * piece, piece)
        y = _dot(c_ref[pl.ds(r0, piece), :], u_all) - _dot(s_ref[pl.ds(r0, piece), :], v_all)
        tiles = slice(k * piece // SUBLANES, (k + 1) * piece // SUBLANES)
        for s in range(nb):
            y_ref[tiles, s] = y[:, s * width:(s + 1) * width].reshape(piece // SUBLANES, SUBLANES, width)

    def from_prev_chunk(v):
        n = v.shape[0] * flat
        r = pltpu.roll(v.reshape(n, hd), 1, 0).reshape(v.shape)
        return jnp.where(sub[None] == 0, 0.0, r)

    def from_next_chunk(v):
        n = v.shape[0] * flat
        r = pltpu.roll(v.reshape(n, hd), n - 1, 0).reshape(v.shape)
        return jnp.where(sub[None] == SUBLANES - 1, 0.0, r)

    def shifted(t0, off, n):
        lo, hi = t0 + off, t0 + off + n
        parts = []
        if lo < 0:
            parts.append(from_prev_chunk(x_ref[lc + lo:lc + min(hi, 0)]))
        if hi > 0 and lo < lc:
            parts.append(x_ref[max(lo, 0):min(hi, lc)])
        if hi > lc:
            parts.append(from_next_chunk(x_ref[max(lo, lc) - lc:hi - lc]))
        return parts[0] if len(parts) == 1 else jnp.concatenate(parts, axis=0)

    z = -lam_ref[...]
    softplus = jnp.maximum(z, 0.0) + jnp.log1p(jnp.exp(-jnp.abs(z)))
    half_l2 = (-0.5 * LRU_C * LOG2_E) * softplus
    half_ba = 0.5 * ba_ref[...]
    half_bx = 0.5 * bx_ref[...]

    for t0 in range(0, lc, ct):
        xc = cb_ref[...][None, None]
        for k in range(CONV_W):
            xc = xc + shifted(t0, k - CONV_LEFT, ct) * cw_ref[k:k + 1, :][None, None]
        xc2 = xc.reshape(ct * flat, hd)
        half_xc = 0.5 * xc2
        g = _dot(xc2.astype(BF16), wg_ref[...])
        fourier_piece(t0 // ct)
        for d in range(2):
            t_r = jnp.tanh(g[:, 2 * d * hd:(2 * d + 1) * hd] + half_ba[d])
            t_i = jnp.tanh(g[:, (2 * d + 1) * hd:(2 * d + 2) * hd] + half_bx[d])
            a = jnp.exp2(half_l2[d] * t_r + half_l2[d])
            m = 1.0 - a * a
            mult = jnp.where(m > 0.0, m * lax.rsqrt(m), 0.0)
            a_s[d, t0:t0 + ct] = a.reshape(ct, *tile)
            b_s[d, t0:t0 + ct] = ((mult * half_xc) * (t_i + 1.0)).reshape(ct, *tile)

    def scan_body(s, carry):
        hf, pf, hb, pb = carry
        tb = lc - 1 - s
        af, bf = a_s[0, s], b_s[0, s]
        ab, bb = a_s[1, tb], b_s[1, tb]
        hf = af * hf + bf
        pf = af * pf
        hb = ab * hb + bb
        pb = ab * pb
        h_s[0, s] = hf
        p_s[0, s] = pf
        h_s[1, tb] = hb
        p_s[1, tb] = pb
        return hf, pf, hb, pb

    zero, one = jnp.zeros(tile, F32), jnp.ones(tile, F32)
    hf, pf, hb, pb = lax.fori_loop(0, lc, scan_body, (zero, one, zero, one), unroll=8)

    def chain(h_end, p_end, h0, forward):
        shift, edge = (1, 0) if forward else (flat - 1, SUBLANES - 1)
        state = h0
        for _ in range(SUBLANES - 1):
            nxt = pltpu.roll((p_end * state + h_end).reshape(flat, hd), shift, 0).reshape(tile)
            state = jnp.where(sub == edge, h0, nxt)
        return state

    sf = chain(hf, pf, h0_ref[0], True)
    sb = chain(hb, pb, h0_ref[1], False)
    st_ref[0] = pf * sf + hf
    st_ref[1] = pb * sb + hb

    for t0 in range(0, lc, ct):
        sl = slice(t0, t0 + ct)
        out_ref[sl] = (h_s[0, sl] + p_s[0, sl] * sf[None]) + (h_s[1, sl] + p_s[1, sl] * sb[None])


def _seq_mix_call(l, lead_block, lc, nb, xr4, uv4, conv_w, conv_b, wg, ba, bx, lam, h0, cmat, smat, ct, name,
                  casts=()):
    n_tiles, n_seq, _, d_rnn = xr4.shape
    hd = d_rnn // N_LRU_HEADS
    assert N_LRU_HEADS == 2 * (uv4.shape[-1] // (2 * MXU_DIM))
    seq_len = lc * SUBLANES
    blk = (lc, nb, SUBLANES, hd)
    n_steps = n_seq // nb * N_LRU_HEADS
    step = lambda b, h: b * N_LRU_HEADS + h
    cast_in_specs, cast_out_specs, cast_out_shapes = [], [], []
    for w, src_layer in casts:
        k, n = w.shape[1:]
        assert k % (n_steps * 2 * SUBLANES) == 0
        cast_in_specs.append(pl.BlockSpec((None, k // n_steps, n), lambda b, h, j=src_layer: (j, step(b, h), 0)))
        cast_out_specs.append(pl.BlockSpec((k // n_steps, n), lambda b, h: (step(b, h), 0)))
        cast_out_shapes.append(jax.ShapeDtypeStruct((k, n), BF16))
    in_specs = [
        pl.BlockSpec(blk, lambda b, h: (lead_block, b, 0, h)),
        pl.BlockSpec((None, CONV_W, hd), lambda b, h: (l, 0, h)),
        pl.BlockSpec((None, 1, hd), lambda b, h: (l, 0, h)),
        pl.BlockSpec((None, None, hd, 4 * hd), lambda b, h: (l, h, 0, 0)),
        pl.BlockSpec((None, 2, 1, hd), lambda b, h: (l, 0, 0, h)),
        pl.BlockSpec((None, 2, 1, hd), lambda b, h: (l, 0, 0, h)),
        pl.BlockSpec((None, 2, 1, hd), lambda b, h: (l, 0, 0, h)),
        pl.BlockSpec((None, 2, nb, SUBLANES, hd), lambda b, h: (l, 0, b, 0, h)),
        pl.BlockSpec((lc, nb, SUBLANES, 2 * MXU_DIM), lambda b, h: (lead_block, b, 0, h // 2)),
        _resident((seq_len, seq_len), lambda b, h: (0, 0)),
        _resident((seq_len, seq_len), lambda b, h: (0, 0)),
    ] + cast_in_specs
    return pl.pallas_call(
        functools.partial(_seq_mix_kernel, ct, len(casts)),
        grid=(n_seq // nb, N_LRU_HEADS),
        in_specs=in_specs,
        out_specs=[
            pl.BlockSpec(blk, lambda b, h: (lead_block, b, 0, h)),
            pl.BlockSpec((2, nb, SUBLANES, hd), lambda b, h: (0, b, 0, h)),
            pl.BlockSpec((lc // 2, nb, SUBLANES, MXU_DIM), lambda b, h: (h % 2, b, 0, h // 2)),
        ] + cast_out_specs,
        out_shape=[
            jax.ShapeDtypeStruct(xr4.shape, F32),
            jax.ShapeDtypeStruct((2, n_seq, SUBLANES, d_rnn), F32),
            jax.ShapeDtypeStruct((lc, n_seq, SUBLANES, uv4.shape[-1] // 2), F32),
        ] + cast_out_shapes,
        scratch_shapes=[
            pltpu.VMEM((2, lc, nb, SUBLANES, hd), F32),
            pltpu.VMEM((2, lc, nb, SUBLANES, hd), F32),
            pltpu.VMEM((2, lc, nb, SUBLANES, hd), F32),
            pltpu.VMEM((2, lc, nb, SUBLANES, hd), F32),
        ],
        input_output_aliases={0: 0},
        compiler_params=_params(("arbitrary", "arbitrary")),
        name=name,
    )(xr4, conv_w, conv_b, wg, ba, bx, lam, h0, uv4, cmat, smat, *[w for w, _ in casts])


def _mix_ffn_kernel(d_ff, lat_tiles, final, split_x, x_ref, *rest):
    xctx_ref = rest[0] if split_x else None
    (rec_ref, gy_ref, ylat_ref, yctx_ref, gate_ref, g1_ref, sh2_ref, sc2_ref, g2_ref, n2_ref,
     wl_ref, wf_ref, wo_ref, w1_ref, w2_ref) = rest[1 if split_x else 0:][:15]
    rest = rest[(1 if split_x else 0) + 15:]
    fin_ref = rest[0] if final else None
    outs, act_s = rest[1 if final else 0:-1], rest[-1]
    d = x_ref.shape[1]
    is_lat = pl.program_id(0) < lat_tiles
    out_a = _dot((rec_ref[...] * gy_ref[...]).astype(BF16), wl_ref[...])
    out_b = _dot(_group_tile(lat_tiles, ylat_ref, yctx_ref).astype(BF16), wf_ref[...])
    merged = gate_ref[:, 0:d] * out_a + gate_ref[:, d:2 * d] * out_b
    x1 = _gated_add(_group_tile(lat_tiles, x_ref, xctx_ref), g1_ref, _dot(merged.astype(BF16), wo_ref[...]))
    h2 = _modulate(_rmsnorm(x1, n2_ref), sc2_ref, sh2_ref).astype(BF16)
    for c0, c1 in _ffn_chunks(d_ff):
        u = _dot(h2, w1_ref[:, c0:c1])
        v = _dot(h2, w1_ref[:, d_ff + c0:d_ff + c1])
        act_s[:, c0:c1] = ((u * _sigmoid(u)) * v).astype(BF16)
    x2 = _gated_add(x1, g2_ref, _dot(act_s[...], w2_ref[...]))
    if fin_ref is None:
        outs[0][...] = x2
    else:
        xn = _rmsnorm(x2, fin_ref)

        @pl.when(is_lat)
        def _():
            outs[0][...] = xn

        @pl.when(jnp.logical_not(is_lat))
        def _():
            outs[1][...] = xn


def _ffn_chunks(d_ff):
    n_tiles = d_ff // MXU_DIM
    half = (n_tiles + 1) // 2 * MXU_DIM
    return [(0, half), (half, d_ff)]


def _mix_ffn_call(l, x, rec, gy, y_lat, y_ctx, gate, mod, norm_g, wl, wf, wo, w1, w2, tm, lat_tiles, final_g):
    xs = x if isinstance(x, tuple) else (x,)
    rows, d = sum(a.shape[0] for a in xs), xs[0].shape[1]
    d_rnn = rec.shape[1]
    d_fnet, d_ff = wf.shape[0], w2.shape[0]
    n_tiles = rows // tm
    grp = lambda i: jnp.where(i < lat_tiles, 0, 1)
    tile = lambda w: pl.BlockSpec((tm, w), lambda i: (i, 0))
    modspec = lambda j: pl.BlockSpec((None, MOD_ROWS, d), lambda i: (l, grp(i), j))
    in_specs = (_group_specs(tm, d, lat_tiles) if len(xs) == 2 else [tile(d)]) + [
        tile(d_rnn), tile(d_rnn), *_group_specs(tm, d_fnet, lat_tiles), tile(2 * d),
        modspec(2), modspec(3), modspec(4), modspec(5),
        pl.BlockSpec((None, 1, d), lambda i: (l, 0, 0)),
    ] + [_resident(w.shape, lambda i: (0, 0)) for w in (wl, wf, wo, w1, w2)]
    args = [*xs, rec, gy, y_lat, y_ctx, gate, mod, mod, mod, mod, norm_g, wl, wf, wo, w1, w2]
    if final_g is None:
        out_specs = tile(d)
        out_shape = jax.ShapeDtypeStruct((rows, d), F32)
    else:
        in_specs.append(pl.BlockSpec((1, d), lambda i: (0, 0)))
        args.append(final_g)
        out_specs = _group_specs(tm, d, lat_tiles)
        out_shape = [
            jax.ShapeDtypeStruct((lat_tiles * tm, d), F32),
            jax.ShapeDtypeStruct(((n_tiles - lat_tiles) * tm, d), F32),
        ]
    return pl.pallas_call(
        functools.partial(_mix_ffn_kernel, d_ff, lat_tiles, final_g is not None, len(xs) == 2),
        grid=(n_tiles,),
        in_specs=in_specs,
        out_specs=out_specs,
        out_shape=out_shape,
        scratch_shapes=[pltpu.VMEM((tm, d_ff), BF16)],
        compiler_params=_params(("arbitrary",)),
        name="mix_ffn",
    )(*args)


def _cos_sin(n):
    k = np.arange(n)
    ang = 2.0 * np.pi * ((k[:, None] * k[None, :]) % n) / n
    return np.cos(ang), np.sin(ang)


def _channel_dft(d_fnet):
    gd = d_fnet // N_FNET_GROUPS
    c, s = _cos_sin(gd)
    eye = np.eye(N_FNET_GROUPS)
    scale = 1.0 / np.sqrt(gd)
    u, v = np.kron(eye, c) * scale, np.kron(eye, s) * scale
    blocks = []
    for j in range(0, d_fnet, MXU_DIM):
        blocks += [u[:, j:j + MXU_DIM], v[:, j:j + MXU_DIM]]
    return np.concatenate(blocks, axis=1)


def _row_order(seq_len):
    j = np.arange(seq_len)
    return (j % SUBLANES) * (seq_len // SUBLANES) + j // SUBLANES


def _seq_dft(n):
    c, s = _cos_sin(n)
    scale = 1.0 / np.sqrt(n)
    p = _row_order(n)
    return (c * scale)[p][:, p], (s * scale)[p][:, p]


def _grid_dft(n_rows, n_cols):
    cr, sr = _cos_sin(n_rows)
    cw, sw = _cos_sin(n_cols)
    scale = 1.0 / np.sqrt(n_rows * n_cols)
    p = _row_order(n_rows * n_cols)
    c = (np.kron(cr, cw) - np.kron(sr, sw)) * scale
    s = (np.kron(sr, cw) + np.kron(cr, sw)) * scale
    return c[p][:, p], s[p][:, p]


def _to_chunk_layout(x):
    b, l, d = x.shape
    return x.reshape(b, SUBLANES, l // SUBLANES, d).transpose(2, 0, 1, 3).reshape(b * l, d)


def _from_chunk_layout(rows, b, l):
    d = rows.shape[-1]
    return rows.reshape(l // SUBLANES, b, SUBLANES, d).transpose(1, 2, 0, 3).reshape(b, l, d)


def kernel(x_prompt, x_sample, state_lru, c, c_ctx, norm1_g, norm2_g, ada_w, ada_b, w_in, b_in, conv_w, conv_b,
           lru_wa, lru_ba, lru_wx, lru_bx, lru_lambda, w_lru_out, w_fnet_out, w_out, ffn_w_in, ffn_w_out,
           final_g):
    n_ctx, l_ctx, d = x_prompt.shape
    n_lat, l_lat, _ = x_sample.shape
    depth = w_in.shape[0]
    d_rnn = conv_w.shape[-1]
    d_fnet = w_fnet_out.shape[1]
    n_in = w_in.shape[-1]
    assert MOD_ROWS % (n_lat * SUBLANES) == 0 and MOD_ROWS % (n_ctx * SUBLANES) == 0
    assert l_lat % GRID_W == 0 and n_in == 2 * d_rnn + d_fnet + 2 * d
    rows_lat, rows_ctx = n_lat * l_lat, n_ctx * l_ctx
    rows = rows_lat + rows_ctx
    assert rows_lat % rows_ctx == 0
    lc_lat, lc_ctx = l_lat // SUBLANES, l_ctx // SUBLANES
    ctx_block = rows_lat // rows_ctx

    tm_in, tm_ffn = 1024, 512
    tm_in_first = 512

    x = (_to_chunk_layout(x_sample), _to_chunk_layout(x_prompt))

    c_pat = jnp.concatenate([
        jnp.tile(jnp.repeat(c, SUBLANES, axis=0), (MOD_ROWS // (n_lat * SUBLANES), 1)),
        jnp.broadcast_to(c_ctx[None, :], (MOD_ROWS, d))], axis=0)
    mod = _ada_call(c_pat, ada_w, ada_b)

    w_in_b = w_in[0].astype(BF16)
    wg = (0.5 * jnp.concatenate([lru_wa[:, 0], lru_wx[:, 0], lru_wa[:, 1], lru_wx[:, 1]], axis=-1)).astype(BF16)
    b_in3 = b_in.reshape(depth, 1, n_in)
    n1 = norm1_g.reshape(depth, 1, d)
    n2 = norm2_g.reshape(depth, 1, d)
    conv_b3 = conv_b.reshape(depth, 1, d_rnn)
    ba4 = lru_ba.reshape(depth, 2, 1, d_rnn)
    bx4 = lru_bx.reshape(depth, 2, 1, d_rnn)
    lam4 = lru_lambda.reshape(depth, 2, 1, d_rnn)
    h0_lat = jnp.broadcast_to(state_lru.transpose(1, 2, 0, 3)[:, :, :, None, :],
                              (depth, 2, n_lat, SUBLANES, d_rnn))
    h0_ctx = jnp.zeros((depth, 2, n_ctx, SUBLANES, d_rnn), F32)

    wc = jnp.asarray(_channel_dft(d_fnet), F32).astype(BF16)
    c_ctx_m, s_ctx_m = (jnp.asarray(m, F32).astype(BF16) for m in _seq_dft(l_ctx))
    c_lat_m, s_lat_m = (jnp.asarray(m, F32).astype(BF16) for m in _grid_dft(l_lat // GRID_W, GRID_W))

    as_lat = lambda a: a.reshape(-1, n_lat, SUBLANES, a.shape[-1])
    as_ctx = lambda a: a.reshape(-1, n_ctx, SUBLANES, a.shape[-1])

    states = []
    for l in range(depth):
        tm = tm_in_first if l == 0 else tm_in
        xr, gy, uv, gate = _inproj_call(l, x, mod, n1, w_in_b, b_in3, wc, tm, rows_lat // tm, d_rnn, d_fnet)

        rec, _, y_lat, w1_b, wl_b, wo_b = _seq_mix_call(l, 0, lc_lat, 1, as_lat(xr), as_lat(uv), conv_w, conv_b3, wg,
                                                        ba4, bx4, lam4, h0_lat, c_lat_m, s_lat_m, 64, "seq_mix_lat",
                                                        [(ffn_w_in, l), (w_lru_out, l), (w_out, l)])
        casts = [(w_fnet_out, l), (ffn_w_out, l)]
        if l + 1 < depth:
            casts.append((w_in, l + 1))
        rec, st, y_ctx, *w_b = _seq_mix_call(l, ctx_block, lc_ctx, 4, as_ctx(rec), as_ctx(uv), conv_w, conv_b3, wg,
                                             ba4, bx4, lam4, h0_ctx, c_ctx_m, s_ctx_m, 16, "seq_mix_ctx", casts)
        states.append(jnp.stack([st[0, :, SUBLANES - 1], st[1, :, 0]], axis=1))

        x = _mix_ffn_call(l, x, rec.reshape(rows, d_rnn), gy, y_lat.reshape(rows_lat, d_fnet),
                          y_ctx.reshape(rows_ctx, d_fnet), gate, mod, n2,
                          wl_b, w_b[0], wo_b, w1_b, w_b[1], tm_ffn, rows_lat // tm_ffn,
                          final_g.reshape(1, d) if l == depth - 1 else None)
        if l + 1 < depth:
            w_in_b = w_b[2]

    y_sample = _from_chunk_layout(x[0], n_lat, l_lat)
    y_prompt = _from_chunk_layout(x[1], n_ctx, l_ctx)
    new_state = jnp.stack(states, axis=1)
    return (y_prompt, y_sample, new_state)
```

```python
import functools

import numpy as np
import jax
import jax.numpy as jnp
from jax import lax
from jax.experimental import pallas as pl
from jax.experimental.pallas import tpu as pltpu

F32 = jnp.float32
BF16 = jnp.bfloat16

GRID_W = 64
N_LRU_HEADS = 4
LRU_C = 8.0
CONV_W = 4
CONV_LEFT = 2
N_FNET_GROUPS = 8
EPS = 1e-6
LOG2_E = 1.4426950408889634

SUBLANES = 8
MXU_DIM = 256
MOD_ROWS = 128
VMEM_LIMIT = 60 * 1024 * 1024


def _sigmoid(x):
    return 0.5 * jnp.tanh(0.5 * x) + 0.5


def _dot(a, b):
    return jnp.dot(a, b, preferred_element_type=F32)


def _params(sem):
    return pltpu.CompilerParams(dimension_semantics=sem, vmem_limit_bytes=VMEM_LIMIT)


def _resident(block_shape, index_map):
    return pl.BlockSpec(block_shape, index_map, pipeline_mode=pl.Buffered(1))


def _modulate(y, scale_ref, shift_ref):
    rows, d = y.shape
    y3 = y.reshape(rows // MOD_ROWS, MOD_ROWS, d)
    y3 = y3 * (1.0 + scale_ref[...])[None] + shift_ref[...][None]
    return y3.reshape(rows, d)


def _gated_add(x, gate_ref, upd):
    rows, d = x.shape
    x3 = x.reshape(rows // MOD_ROWS, MOD_ROWS, d)
    u3 = upd.reshape(rows // MOD_ROWS, MOD_ROWS, d)
    return (x3 + gate_ref[...][None] * u3).reshape(rows, d)


def _rmsnorm(x, g_ref):
    ms = jnp.mean(x * x, axis=-1, keepdims=True)
    return x * lax.rsqrt(ms + EPS) * g_ref[...]


def _ada_kernel(c_ref, w_ref, b_ref, o_ref):
    @pl.when(pl.program_id(1) == 0)
    def _():
        o_ref[...] = jnp.broadcast_to(b_ref[...], o_ref.shape)

    c = c_ref[...]
    cs = c * _sigmoid(c)
    o_ref[...] += _dot(cs.astype(BF16), w_ref[...].astype(BF16))


def _ada_call(c_pat, ada_w, ada_b):
    depth, d, n6 = ada_w.shape
    rows = c_pat.shape[0]
    tk = MXU_DIM
    return pl.pallas_call(
        _ada_kernel,
        grid=(depth, d // tk),
        in_specs=[
            pl.BlockSpec((rows, tk), lambda l, k: (0, k)),
            pl.BlockSpec((None, tk, n6), lambda l, k: (l, k, 0)),
            pl.BlockSpec((None, 1, n6), lambda l, k: (l, 0, 0)),
        ],
        out_specs=pl.BlockSpec((None, rows, n6), lambda l, k: (l, 0, 0)),
        out_shape=jax.ShapeDtypeStruct((depth, rows, n6), F32),
        compiler_params=_params(("arbitrary", "arbitrary")),
        name="ada_mod",
    )(c_pat, ada_w, ada_b.reshape(depth, 1, n6))


def _group_tile(lat_tiles, lat_ref, ctx_ref):
    if ctx_ref is None:
        return lat_ref[...]
    return jnp.where(pl.program_id(0) < lat_tiles, lat_ref[...], ctx_ref[...])


def _group_specs(tm, width, lat_tiles):
    return [pl.BlockSpec((tm, width), lambda i: (jnp.minimum(i, lat_tiles - 1), 0)),
            pl.BlockSpec((tm, width), lambda i: (jnp.maximum(i - lat_tiles, 0), 0))]


def _inproj_kernel(d_rnn, d_fnet, lat_tiles, split_x, x_ref, *rest):
    xctx_ref = rest[0] if split_x else None
    sh_ref, sc_ref, g_ref, w_ref, b_ref, wc_ref, xr_ref, gy_ref, uv_ref, gate_ref = rest[1 if split_x else 0:]
    h = _modulate(_rmsnorm(_group_tile(lat_tiles, x_ref, xctx_ref), g_ref), sc_ref, sh_ref).astype(BF16)
    o1, o2, o3 = d_rnn, 2 * d_rnn, 2 * d_rnn + d_fnet
    n_in = w_ref.shape[1]
    xr_ref[...] = _dot(h, w_ref[:, 0:o1]) + b_ref[:, 0:o1]
    gy_ref[...] = jax.nn.gelu(_dot(h, w_ref[:, o1:o2]) + b_ref[:, o1:o2]).astype(BF16)
    xf = _dot(h, w_ref[:, o2:o3]) + b_ref[:, o2:o3]
    uv_ref[...] = _dot(xf.astype(BF16), wc_ref[...])
    gate_ref[...] = _sigmoid(_dot(h, w_ref[:, o3:n_in]) + b_ref[:, o3:n_in]).astype(BF16)


def _inproj_call(l, x, mod, norm_g, w_in, b_in, wc, tm, lat_tiles, d_rnn, d_fnet):
    xs = x if isinstance(x, tuple) else (x,)
    rows, d = sum(a.shape[0] for a in xs), xs[0].shape[1]
    n_in = w_in.shape[1]
    n_gate = n_in - 2 * d_rnn - d_fnet
    grp = lambda i: jnp.where(i < lat_tiles, 0, 1)
    x_specs = _group_specs(tm, d, lat_tiles) if len(xs) == 2 else [pl.BlockSpec((tm, d), lambda i: (i, 0))]
    return pl.pallas_call(
        functools.partial(_inproj_kernel, d_rnn, d_fnet, lat_tiles, len(xs) == 2),
        grid=(rows // tm,),
        in_specs=x_specs + [
            pl.BlockSpec((None, MOD_ROWS, d), lambda i: (l, grp(i), 0)),
            pl.BlockSpec((None, MOD_ROWS, d), lambda i: (l, grp(i), 1)),
            pl.BlockSpec((None, 1, d), lambda i: (l, 0, 0)),
            _resident((d, n_in), lambda i: (0, 0)),
            pl.BlockSpec((None, 1, n_in), lambda i: (l, 0, 0)),
            _resident((d_fnet, 2 * d_fnet), lambda i: (0, 0)),
        ],
        out_specs=[
            pl.BlockSpec((tm, d_rnn), lambda i: (i, 0)),
            pl.BlockSpec((tm, d_rnn), lambda i: (i, 0)),
            pl.BlockSpec((tm, 2 * d_fnet), lambda i: (i, 0)),
            pl.BlockSpec((tm, n_gate), lambda i: (i, 0)),
        ],
        out_shape=[
            jax.ShapeDtypeStruct((rows, d_rnn), F32),
            jax.ShapeDtypeStruct((rows, d_rnn), BF16),
            jax.ShapeDtypeStruct((rows, 2 * d_fnet), F32),
            jax.ShapeDtypeStruct((rows, n_gate), BF16),
        ],
        compiler_params=_params(("arbitrary",)),
        name="in_proj",
    )(*xs, mod, mod, norm_g, w_in, b_in, wc)


def _seq_mix_kernel(ct, n_cast, x_ref, par_ref, wg_ref, h0_ref, uv_ref, c_ref, s_ref, *rest):
    cast_in, (out_ref, st_ref, y_ref), cast_out = rest[:n_cast], rest[n_cast:n_cast + 3], rest[n_cast + 3:-4]
    a_s, b_s, h_s, p_s = rest[-4:]
    par = par_ref[pl.program_id(1)]
    cw = par[0:CONV_W]
    cb = par[CONV_W:CONV_W + 1]
    ba, bx, lam = (par[CONV_W + 1 + 2 * j:CONV_W + 3 + 2 * j] for j in range(3))
    for w_ref, wb_ref in zip(cast_in, cast_out):
        wb_ref[...] = w_ref[...].astype(BF16)
    lc, nb, _, hd = x_ref.shape
    tile = (nb, SUBLANES, hd)
    flat = nb * SUBLANES
    sub = lax.broadcasted_iota(jnp.int32, tile, 1)

    width = y_ref.shape[-1]
    seq_rows = lc * SUBLANES
    n_chunks = lc // ct
    piece = seq_rows // 2 // n_chunks
    side_by_side = lambda lo: jnp.concatenate(
        [uv_ref[:, s, :, lo:lo + width].reshape(seq_rows, width) for s in range(nb)], axis=1).astype(BF16)
    u_all, v_all = side_by_side(0), side_by_side(width)

    def fourier_piece(k):
        r0 = pl.multiple_of((pl.program_id(1) % 2) * (seq_rows // 2) + k * piece, piece)
        y = _dot(c_ref[pl.ds(r0, piece), :], u_all) - _dot(s_ref[pl.ds(r0, piece), :], v_all)
        tiles = slice(k * piece // SUBLANES, (k + 1) * piece // SUBLANES)
        for s in range(nb):
            y_ref[tiles, s] = y[:, s * width:(s + 1) * width].reshape(piece // SUBLANES, SUBLANES, width)

    def from_prev_chunk(v):
        n = v.shape[0] * flat
        r = pltpu.roll(v.reshape(n, hd), 1, 0).reshape(v.shape)
        return jnp.where(sub[None] == 0, 0.0, r)

    def from_next_chunk(v):
        n = v.shape[0] * flat
        r = pltpu.roll(v.reshape(n, hd), n - 1, 0).reshape(v.shape)
        return jnp.where(sub[None] == SUBLANES - 1, 0.0, r)

    def shifted(t0, off, n):
        lo, hi = t0 + off, t0 + off + n
        parts = []
        if lo < 0:
            parts.append(from_prev_chunk(x_ref[lc + lo:lc + min(hi, 0)]))
        if hi > 0 and lo < lc:
            parts.append(x_ref[max(lo, 0):min(hi, lc)])
        if hi > lc:
            parts.append(from_next_chunk(x_ref[max(lo, lc) - lc:hi - lc]))
        return parts[0] if len(parts) == 1 else jnp.concatenate(parts, axis=0)

    z = -lam
    softplus = jnp.maximum(z, 0.0) + jnp.log1p(jnp.exp(-jnp.abs(z)))
    half_l2 = (-0.5 * LRU_C * LOG2_E) * softplus
    half_ba = 0.5 * ba
    half_bx = 0.5 * bx

    for t0 in range(0, lc, ct):
        xc = cb[None, None]
        for k in range(CONV_W):
            xc = xc + shifted(t0, k - CONV_LEFT, ct) * cw[k:k + 1, :][None, None]
        xc2 = xc.reshape(ct * flat, hd)
        half_xc = 0.5 * xc2
        g = _dot(xc2.astype(BF16), wg_ref[...])
        fourier_piece(t0 // ct)
        for d in range(2):
            t_r = jnp.tanh(g[:, 2 * d * hd:(2 * d + 1) * hd] + half_ba[d:d + 1])
            t_i = jnp.tanh(g[:, (2 * d + 1) * hd:(2 * d + 2) * hd] + half_bx[d:d + 1])
            a = jnp.exp2(half_l2[d:d + 1] * t_r + half_l2[d:d + 1])
            m = 1.0 - a * a
            mult = jnp.where(m > 0.0, m * lax.rsqrt(m), 0.0)
            a_s[d, t0:t0 + ct] = a.reshape(ct, *tile)
            b_s[d, t0:t0 + ct] = ((mult * half_xc) * (t_i + 1.0)).reshape(ct, *tile)

    def scan_body(s, carry):
        hf, pf, hb, pb = carry
        tb = lc - 1 - s
        af, bf = a_s[0, s], b_s[0, s]
        ab, bb = a_s[1, tb], b_s[1, tb]
        hf = af * hf + bf
        pf = af * pf
        hb = ab * hb + bb
        pb = ab * pb
        h_s[0, s] = hf
        p_s[0, s] = pf
        h_s[1, tb] = hb
        p_s[1, tb] = pb
        return hf, pf, hb, pb

    zero, one = jnp.zeros(tile, F32), jnp.ones(tile, F32)
    hf, pf, hb, pb = lax.fori_loop(0, lc, scan_body, (zero, one, zero, one), unroll=8)

    def chain(h_end, p_end, h0, forward):
        shift, edge = (1, 0) if forward else (flat - 1, SUBLANES - 1)
        state = h0
        for _ in range(SUBLANES - 1):
            nxt = pltpu.roll((p_end * state + h_end).reshape(flat, hd), shift, 0).reshape(tile)
            state = jnp.where(sub == edge, h0, nxt)
        return state

    sf = chain(hf, pf, h0_ref[0], True)
    sb = chain(hb, pb, h0_ref[1], False)
    st_ref[0] = pf * sf + hf
    st_ref[1] = pb * sb + hb

    for t0 in range(0, lc, ct):
        sl = slice(t0, t0 + ct)
        out_ref[sl] = (h_s[0, sl] + p_s[0, sl] * sf[None]) + (h_s[1, sl] + p_s[1, sl] * sb[None])


def _seq_mix_call(l, lead_block, lc, nb, xr4, uv4, head_par, wg, h0, cmat, smat, ct, name, casts=()):
    n_tiles, n_seq, _, d_rnn = xr4.shape
    hd = d_rnn // N_LRU_HEADS
    assert N_LRU_HEADS == 2 * (uv4.shape[-1] // (2 * MXU_DIM))
    seq_len = lc * SUBLANES
    blk = (lc, nb, SUBLANES, hd)
    n_steps = n_seq // nb * N_LRU_HEADS
    step = lambda b, h: b * N_LRU_HEADS + h
    cast_in_specs, cast_out_specs, cast_out_shapes = [], [], []
    for w, src_layer in casts:
        k, n = w.shape[1:]
        assert k % (n_steps * 2 * SUBLANES) == 0
        cast_in_specs.append(pl.BlockSpec((None, k // n_steps, n), lambda b, h, j=src_layer: (j, step(b, h), 0)))
        cast_out_specs.append(pl.BlockSpec((k // n_steps, n), lambda b, h: (step(b, h), 0)))
        cast_out_shapes.append(jax.ShapeDtypeStruct((k, n), BF16))
    in_specs = [
        pl.BlockSpec(blk, lambda b, h: (lead_block, b, 0, h)),
        _resident((None,) + head_par.shape[1:], lambda b, h: (l, 0, 0, 0)),
        pl.BlockSpec((None, None) + wg.shape[2:], lambda b, h: (l, h, 0, 0)),
        pl.BlockSpec((None, 2, nb, SUBLANES, hd), lambda b, h: (l, 0, b, 0, h)),
        pl.BlockSpec((lc, nb, SUBLANES, 2 * MXU_DIM), lambda b, h: (lead_block, b, 0, h // 2)),
        _resident((seq_len, seq_len), lambda b, h: (0, 0)),
        _resident((seq_len, seq_len), lambda b, h: (0, 0)),
    ] + cast_in_specs
    return pl.pallas_call(
        functools.partial(_seq_mix_kernel, ct, len(casts)),
        grid=(n_seq // nb, N_LRU_HEADS),
        in_specs=in_specs,
        out_specs=[
            pl.BlockSpec(blk, lambda b, h: (lead_block, b, 0, h)),
            pl.BlockSpec((2, nb, SUBLANES, hd), lambda b, h: (0, b, 0, h)),
            pl.BlockSpec((lc // 2, nb, SUBLANES, MXU_DIM), lambda b, h: (h % 2, b, 0, h // 2)),
        ] + cast_out_specs,
        out_shape=[
            jax.ShapeDtypeStruct(xr4.shape, F32),
            jax.ShapeDtypeStruct((2, n_seq, SUBLANES, d_rnn), F32),
            jax.ShapeDtypeStruct((lc, n_seq, SUBLANES, uv4.shape[-1] // 2), F32),
        ] + cast_out_shapes,
        scratch_shapes=[
            pltpu.VMEM((2, lc, nb, SUBLANES, hd), F32),
            pltpu.VMEM((2, lc, nb, SUBLANES, hd), F32),
            pltpu.VMEM((2, lc, nb, SUBLANES, hd), F32),
            pltpu.VMEM((2, lc, nb, SUBLANES, hd), F32),
        ],
        input_output_aliases={0: 0},
        compiler_params=_params(("arbitrary", "arbitrary")),
        name=name,
    )(xr4, head_par, wg, h0, uv4, cmat, smat, *[w for w, _ in casts])


def _mix_ffn_kernel(d_ff, lat_tiles, final, split_x, x_ref, *rest):
    xctx_ref = rest[0] if split_x else None
    (rec_ref, gy_ref, ylat_ref, yctx_ref, gate_ref, g1_ref, sh2_ref, sc2_ref, g2_ref, n2_ref,
     wl_ref, wf_ref, wo_ref, w1_ref, w2_ref) = rest[1 if split_x else 0:][:15]
    rest = rest[(1 if split_x else 0) + 15:]
    fin_ref = rest[0] if final else None
    outs, act_s = rest[1 if final else 0:-1], rest[-1]
    d = x_ref.shape[1]
    is_lat = pl.program_id(0) < lat_tiles
    out_a = _dot((rec_ref[...] * gy_ref[...]).astype(BF16), wl_ref[...])
    out_b = _dot(_group_tile(lat_tiles, ylat_ref, yctx_ref).astype(BF16), wf_ref[...])
    merged = gate_ref[:, 0:d] * out_a + gate_ref[:, d:2 * d] * out_b
    x1 = _gated_add(_group_tile(lat_tiles, x_ref, xctx_ref), g1_ref, _dot(merged.astype(BF16), wo_ref[...]))
    h2 = _modulate(_rmsnorm(x1, n2_ref), sc2_ref, sh2_ref).astype(BF16)
    for c0, c1 in _ffn_chunks(d_ff):
        u = _dot(h2, w1_ref[:, c0:c1])
        v = _dot(h2, w1_ref[:, d_ff + c0:d_ff + c1])
        act_s[:, c0:c1] = ((u * _sigmoid(u)) * v).astype(BF16)
    x2 = _gated_add(x1, g2_ref, _dot(act_s[...], w2_ref[...]))
    if fin_ref is None:
        outs[0][...] = x2
    else:
        xn = _rmsnorm(x2, fin_ref)

        @pl.when(is_lat)
        def _():
            outs[0][...] = xn

        @pl.when(jnp.logical_not(is_lat))
        def _():
            outs[1][...] = xn


def _ffn_chunks(d_ff):
    n_tiles = d_ff // MXU_DIM
    half = (n_tiles + 1) // 2 * MXU_DIM
    return [(0, half), (half, d_ff)]


def _mix_ffn_call(l, x, rec, gy, y_lat, y_ctx, gate, mod, norm_g, wl, wf, wo, w1, w2, tm, lat_tiles, final_g):
    xs = x if isinstance(x, tuple) else (x,)
    rows, d = sum(a.shape[0] for a in xs), xs[0].shape[1]
    d_rnn = rec.shape[1]
    d_fnet, d_ff = wf.shape[0], w2.shape[0]
    n_tiles = rows // tm
    grp = lambda i: jnp.where(i < lat_tiles, 0, 1)
    tile = lambda w: pl.BlockSpec((tm, w), lambda i: (i, 0))
    modspec = lambda j: pl.BlockSpec((None, MOD_ROWS, d), lambda i: (l, grp(i), j))
    in_specs = (_group_specs(tm, d, lat_tiles) if len(xs) == 2 else [tile(d)]) + [
        tile(d_rnn), tile(d_rnn), *_group_specs(tm, d_fnet, lat_tiles), tile(2 * d),
        modspec(2), modspec(3), modspec(4), modspec(5),
        pl.BlockSpec((None, 1, d), lambda i: (l, 0, 0)),
    ] + [_resident(w.shape, lambda i: (0, 0)) for w in (wl, wf, wo, w1, w2)]
    args = [*xs, rec, gy, y_lat, y_ctx, gate, mod, mod, mod, mod, norm_g, wl, wf, wo, w1, w2]
    if final_g is None:
        out_specs = tile(d)
        out_shape = jax.ShapeDtypeStruct((rows, d), F32)
    else:
        in_specs.append(pl.BlockSpec((1, d), lambda i: (0, 0)))
        args.append(final_g)
        out_specs = _group_specs(tm, d, lat_tiles)
        out_shape = [
            jax.ShapeDtypeStruct((lat_tiles * tm, d), F32),
            jax.ShapeDtypeStruct(((n_tiles - lat_tiles) * tm, d), F32),
        ]
    return pl.pallas_call(
        functools.partial(_mix_ffn_kernel, d_ff, lat_tiles, final_g is not None, len(xs) == 2),
        grid=(n_tiles,),
        in_specs=in_specs,
        out_specs=out_specs,
        out_shape=out_shape,
        scratch_shapes=[pltpu.VMEM((tm, d_ff), BF16)],
        compiler_params=_params(("arbitrary",)),
        name="mix_ffn",
    )(*args)


def _cos_sin(n):
    k = np.arange(n)
    ang = 2.0 * np.pi * ((k[:, None] * k[None, :]) % n) / n
    return np.cos(ang), np.sin(ang)


def _channel_dft(d_fnet):
    gd = d_fnet // N_FNET_GROUPS
    c, s = _cos_sin(gd)
    eye = np.eye(N_FNET_GROUPS)
    scale = 1.0 / np.sqrt(gd)
    u, v = np.kron(eye, c) * scale, np.kron(eye, s) * scale
    blocks = []
    for j in range(0, d_fnet, MXU_DIM):
        blocks += [u[:, j:j + MXU_DIM], v[:, j:j + MXU_DIM]]
    return np.concatenate(blocks, axis=1)


def _row_order(seq_len):
    j = np.arange(seq_len)
    return (j % SUBLANES) * (seq_len // SUBLANES) + j // SUBLANES


def _seq_dft(n):
    c, s = _cos_sin(n)
    scale = 1.0 / np.sqrt(n)
    p = _row_order(n)
    return (c * scale)[p][:, p], (s * scale)[p][:, p]


def _grid_dft(n_rows, n_cols):
    cr, sr = _cos_sin(n_rows)
    cw, sw = _cos_sin(n_cols)
    scale = 1.0 / np.sqrt(n_rows * n_cols)
    p = _row_order(n_rows * n_cols)
    c = (np.kron(cr, cw) - np.kron(sr, sw)) * scale
    s = (np.kron(sr, cw) + np.kron(cr, sw)) * scale
    return c[p][:, p], s[p][:, p]


def _to_chunk_layout(x):
    b, l, d = x.shape
    return x.reshape(b, SUBLANES, l // SUBLANES, d).transpose(2, 0, 1, 3).reshape(b * l, d)


def _from_chunk_layout(rows, b, l):
    d = rows.shape[-1]
    return rows.reshape(l // SUBLANES, b, SUBLANES, d).transpose(1, 2, 0, 3).reshape(b, l, d)


def kernel(x_prompt, x_sample, state_lru, c, c_ctx, norm1_g, norm2_g, ada_w, ada_b, w_in, b_in, conv_w, conv_b,
           lru_wa, lru_ba, lru_wx, lru_bx, lru_lambda, w_lru_out, w_fnet_out, w_out, ffn_w_in, ffn_w_out,
           final_g):
    n_ctx, l_ctx, d = x_prompt.shape
    n_lat, l_lat, _ = x_sample.shape
    depth = w_in.shape[0]
    d_rnn = conv_w.shape[-1]
    d_fnet = w_fnet_out.shape[1]
    n_in = w_in.shape[-1]
    assert MOD_ROWS % (n_lat * SUBLANES) == 0 and MOD_ROWS % (n_ctx * SUBLANES) == 0
    assert l_lat % GRID_W == 0 and n_in == 2 * d_rnn + d_fnet + 2 * d
    rows_lat, rows_ctx = n_lat * l_lat, n_ctx * l_ctx
    rows = rows_lat + rows_ctx
    assert rows_lat % rows_ctx == 0
    lc_lat, lc_ctx = l_lat // SUBLANES, l_ctx // SUBLANES
    ctx_block = rows_lat // rows_ctx

    tm_in, tm_ffn = 1024, 512
    tm_in_first = 512

    x = (_to_chunk_layout(x_sample), _to_chunk_layout(x_prompt))

    c_pat = jnp.concatenate([
        jnp.tile(jnp.repeat(c, SUBLANES, axis=0), (MOD_ROWS // (n_lat * SUBLANES), 1)),
        jnp.broadcast_to(c_ctx[None, :], (MOD_ROWS, d))], axis=0)
    mod = _ada_call(c_pat, ada_w, ada_b)

    w_in_b = w_in[0].astype(BF16)
    wg = (0.5 * jnp.concatenate([lru_wa[:, 0], lru_wx[:, 0], lru_wa[:, 1], lru_wx[:, 1]], axis=-1)).astype(BF16)
    n_par = CONV_W + 1 + 6
    head_par = jnp.concatenate([conv_w, conv_b[:, None, :], lru_ba, lru_bx, lru_lambda,
                                jnp.zeros((depth, 2 * SUBLANES - n_par, d_rnn), F32)], axis=1)
    head_par = head_par.reshape(depth, 2 * SUBLANES, N_LRU_HEADS, d_rnn // N_LRU_HEADS).transpose(0, 2, 1, 3)
    b_in3 = b_in.reshape(depth, 1, n_in)
    n1 = norm1_g.reshape(depth, 1, d)
    n2 = norm2_g.reshape(depth, 1, d)
    h0_lat = jnp.broadcast_to(state_lru.transpose(1, 2, 0, 3)[:, :, :, None, :],
                              (depth, 2, n_lat, SUBLANES, d_rnn))
    h0_ctx = jnp.zeros((depth, 2, n_ctx, SUBLANES, d_rnn), F32)

    wc = jnp.asarray(_channel_dft(d_fnet), F32).astype(BF16)
    c_ctx_m, s_ctx_m = (jnp.asarray(m, F32).astype(BF16) for m in _seq_dft(l_ctx))
    c_lat_m, s_lat_m = (jnp.asarray(m, F32).astype(BF16) for m in _grid_dft(l_lat // GRID_W, GRID_W))

    as_lat = lambda a: a.reshape(-1, n_lat, SUBLANES, a.shape[-1])
    as_ctx = lambda a: a.reshape(-1, n_ctx, SUBLANES, a.shape[-1])

    states = []
    for l in range(depth):
        tm = tm_in_first if l == 0 else tm_in
        xr, gy, uv, gate = _inproj_call(l, x, mod, n1, w_in_b, b_in3, wc, tm, rows_lat // tm, d_rnn, d_fnet)

        rec, _, y_lat, w1_b, wl_b, wo_b = _seq_mix_call(l, 0, lc_lat, 1, as_lat(xr), as_lat(uv), head_par, wg,
                                                        h0_lat, c_lat_m, s_lat_m, 64, "seq_mix_lat",
                                                        [(ffn_w_in, l), (w_lru_out, l), (w_out, l)])
        casts = [(w_fnet_out, l), (ffn_w_out, l)]
        if l + 1 < depth:
            casts.append((w_in, l + 1))
        rec, st, y_ctx, *w_b = _seq_mix_call(l, ctx_block, lc_ctx, 8, as_ctx(rec), as_ctx(uv), head_par, wg,
                                             h0_ctx, c_ctx_m, s_ctx_m, 8, "seq_mix_ctx", casts)
        states.append(jnp.stack([st[0, :, SUBLANES - 1], st[1, :, 0]], axis=1))

        x = _mix_ffn_call(l, x, rec.reshape(rows, d_rnn), gy, y_lat.reshape(rows_lat, d_fnet),
                          y_ctx.reshape(rows_ctx, d_fnet), gate, mod, n2,
                          wl_b, w_b[0], wo_b, w1_b, w_b[1], tm_ffn, rows_lat // tm_ffn,
                          final_g.reshape(1, d) if l == depth - 1 else None)
        if l + 1 < depth:
            w_in_b = w_b[2]

    y_sample = _from_chunk_layout(x[0], n_lat, l_lat)
    y_prompt = _from_chunk_layout(x[1], n_ctx, l_ctx)
    new_state = jnp.stack(states, axis=1)
    return (y_prompt, y_sample, new_state)
```

```python
import functools

import numpy as np
import jax
import jax.numpy as jnp
from jax import lax
from jax.experimental import pallas as pl
from jax.experimental.pallas import tpu as pltpu

F32 = jnp.float32
BF16 = jnp.bfloat16

GRID_W = 64
N_LRU_HEADS = 4
LRU_C = 8.0
CONV_W = 4
CONV_LEFT = 2
N_FNET_GROUPS = 8
EPS = 1e-6
LOG2_E = 1.4426950408889634

SUBLANES = 8
MXU_DIM = 256
MOD_ROWS = 128
VMEM_LIMIT = 60 * 1024 * 1024


def _sigmoid(x):
    return 0.5 * jnp.tanh(0.5 * x) + 0.5


def _dot(a, b):
    return jnp.dot(a, b, preferred_element_type=F32)


def _params(sem):
    return pltpu.CompilerParams(dimension_semantics=sem, vmem_limit_bytes=VMEM_LIMIT)


def _resident(block_shape, index_map):
    return pl.BlockSpec(block_shape, index_map, pipeline_mode=pl.Buffered(1))


def _modulate(y, scale_ref, shift_ref):
    rows, d = y.shape
    y3 = y.reshape(rows // MOD_ROWS, MOD_ROWS, d)
    y3 = y3 * (1.0 + scale_ref[...])[None] + shift_ref[...][None]
    return y3.reshape(rows, d)


def _gated_add(x, gate_ref, upd):
    rows, d = x.shape
    x3 = x.reshape(rows // MOD_ROWS, MOD_ROWS, d)
    u3 = upd.reshape(rows // MOD_ROWS, MOD_ROWS, d)
    return (x3 + gate_ref[...][None] * u3).reshape(rows, d)


def _rmsnorm(x, g_ref):
    ms = jnp.mean(x * x, axis=-1, keepdims=True)
    return x * lax.rsqrt(ms + EPS) * g_ref[...]


def _ada_kernel(c_ref, w_ref, b_ref, o_ref):
    @pl.when(pl.program_id(1) == 0)
    def _():
        o_ref[...] = jnp.broadcast_to(b_ref[...], o_ref.shape)

    c = c_ref[...]
    cs = c * _sigmoid(c)
    o_ref[...] += _dot(cs.astype(BF16), w_ref[...].astype(BF16))


def _ada_call(c_pat, ada_w, ada_b):
    depth, d, n6 = ada_w.shape
    rows = c_pat.shape[0]
    tk = MXU_DIM
    return pl.pallas_call(
        _ada_kernel,
        grid=(depth, d // tk),
        in_specs=[
            pl.BlockSpec((rows, tk), lambda l, k: (0, k)),
            pl.BlockSpec((None, tk, n6), lambda l, k: (l, k, 0)),
            pl.BlockSpec((None, 1, n6), lambda l, k: (l, 0, 0)),
        ],
        out_specs=pl.BlockSpec((None, rows, n6), lambda l, k: (l, 0, 0)),
        out_shape=jax.ShapeDtypeStruct((depth, rows, n6), F32),
        compiler_params=_params(("arbitrary", "arbitrary")),
        name="ada_mod",
    )(c_pat, ada_w, ada_b.reshape(depth, 1, n6))


def _group_tile(lat_tiles, lat_ref, ctx_ref):
    if ctx_ref is None:
        return lat_ref[...]
    return jnp.where(pl.program_id(0) < lat_tiles, lat_ref[...], ctx_ref[...])


def _group_specs(tm, width, lat_tiles):
    return [pl.BlockSpec((tm, width), lambda i: (jnp.minimum(i, lat_tiles - 1), 0)),
            pl.BlockSpec((tm, width), lambda i: (jnp.maximum(i - lat_tiles, 0), 0))]


def _inproj_kernel(d_rnn, d_fnet, lat_tiles, split_x, x_ref, *rest):
    xctx_ref = rest[0] if split_x else None
    sh_ref, sc_ref, g_ref, w_ref, b_ref, wc_ref, xr_ref, gy_ref, uv_ref, gate_ref = rest[1 if split_x else 0:]
    h = _modulate(_rmsnorm(_group_tile(lat_tiles, x_ref, xctx_ref), g_ref), sc_ref, sh_ref).astype(BF16)
    o1, o2, o3 = d_rnn, 2 * d_rnn, 2 * d_rnn + d_fnet
    n_in = w_ref.shape[1]
    xr_ref[...] = _dot(h, w_ref[:, 0:o1]) + b_ref[:, 0:o1]
    gy_ref[...] = jax.nn.gelu(_dot(h, w_ref[:, o1:o2]) + b_ref[:, o1:o2]).astype(BF16)
    xf = _dot(h, w_ref[:, o2:o3]) + b_ref[:, o2:o3]
    uv_ref[...] = _dot(xf.astype(BF16), wc_ref[...])
    gate_ref[...] = _sigmoid(_dot(h, w_ref[:, o3:n_in]) + b_ref[:, o3:n_in]).astype(BF16)


def _inproj_call(l, x, mod, norm_g, w_in, b_in, wc, tm, lat_tiles, d_rnn, d_fnet):
    xs = x if isinstance(x, tuple) else (x,)
    rows, d = sum(a.shape[0] for a in xs), xs[0].shape[1]
    n_in = w_in.shape[1]
    n_gate = n_in - 2 * d_rnn - d_fnet
    grp = lambda i: jnp.where(i < lat_tiles, 0, 1)
    x_specs = _group_specs(tm, d, lat_tiles) if len(xs) == 2 else [pl.BlockSpec((tm, d), lambda i: (i, 0))]
    return pl.pallas_call(
        functools.partial(_inproj_kernel, d_rnn, d_fnet, lat_tiles, len(xs) == 2),
        grid=(rows // tm,),
        in_specs=x_specs + [
            pl.BlockSpec((None, MOD_ROWS, d), lambda i: (l, grp(i), 0)),
            pl.BlockSpec((None, MOD_ROWS, d), lambda i: (l, grp(i), 1)),
            pl.BlockSpec((None, 1, d), lambda i: (l, 0, 0)),
            _resident((d, n_in), lambda i: (0, 0)),
            pl.BlockSpec((None, 1, n_in), lambda i: (l, 0, 0)),
            _resident((d_fnet, 2 * d_fnet), lambda i: (0, 0)),
        ],
        out_specs=[
            pl.BlockSpec((tm, d_rnn), lambda i: (i, 0)),
            pl.BlockSpec((tm, d_rnn), lambda i: (i, 0)),
            pl.BlockSpec((tm, 2 * d_fnet), lambda i: (i, 0)),
            pl.BlockSpec((tm, n_gate), lambda i: (i, 0)),
        ],
        out_shape=[
            jax.ShapeDtypeStruct((rows, d_rnn), F32),
            jax.ShapeDtypeStruct((rows, d_rnn), BF16),
            jax.ShapeDtypeStruct((rows, 2 * d_fnet), F32),
            jax.ShapeDtypeStruct((rows, n_gate), BF16),
        ],
        compiler_params=_params(("arbitrary",)),
        name="in_proj",
    )(*xs, mod, mod, norm_g, w_in, b_in, wc)


def _seq_mix_kernel(ct, n_cast, x_ref, par_ref, wg_ref, h0_ref, uv_ref, c_ref, s_ref, *rest):
    cast_in, (out_ref, st_ref, y_ref), cast_out = rest[:n_cast], rest[n_cast:n_cast + 3], rest[n_cast + 3:-4]
    a_s, b_s, h_s, p_s = rest[-4:]
    par = par_ref[pl.program_id(1)]
    cw = par[0:CONV_W]
    cb = par[CONV_W:CONV_W + 1]
    ba, bx, lam = (par[CONV_W + 1 + 2 * j:CONV_W + 3 + 2 * j] for j in range(3))
    for w_ref, wb_ref in zip(cast_in, cast_out):
        wb_ref[...] = w_ref[...].astype(BF16)
    lc, nb, _, hd = x_ref.shape
    tile = (nb, SUBLANES, hd)
    flat = nb * SUBLANES
    sub = lax.broadcasted_iota(jnp.int32, tile, 1)

    width = y_ref.shape[-1]
    seq_rows = lc * SUBLANES
    n_chunks = lc // ct
    piece = seq_rows // 2 // n_chunks
    side_by_side = lambda lo: jnp.concatenate(
        [uv_ref[:, s, :, lo:lo + width].reshape(seq_rows, width) for s in range(nb)], axis=1).astype(BF16)
    u_all, v_all = side_by_side(0), side_by_side(width)

    def fourier_piece(k):
        r0 = pl.multiple_of((pl.program_id(1) % 2) * (seq_rows // 2) + k * piece, piece)
        y = _dot(c_ref[pl.ds(r0, piece), :], u_all) - _dot(s_ref[pl.ds(r0, piece), :], v_all)
        tiles = slice(k * piece // SUBLANES, (k + 1) * piece // SUBLANES)
        for s in range(nb):
            y_ref[tiles, s] = y[:, s * width:(s + 1) * width].reshape(piece // SUBLANES, SUBLANES, width)

    def from_prev_chunk(v):
        n = v.shape[0] * flat
        r = pltpu.roll(v.reshape(n, hd), 1, 0).reshape(v.shape)
        return jnp.where(sub[None] == 0, 0.0, r)

    def from_next_chunk(v):
        n = v.shape[0] * flat
        r = pltpu.roll(v.reshape(n, hd), n - 1, 0).reshape(v.shape)
        return jnp.where(sub[None] == SUBLANES - 1, 0.0, r)

    def shifted(t0, off, n):
        lo, hi = t0 + off, t0 + off + n
        parts = []
        if lo < 0:
            parts.append(from_prev_chunk(x_ref[lc + lo:lc + min(hi, 0)]))
        if hi > 0 and lo < lc:
            parts.append(x_ref[max(lo, 0):min(hi, lc)])
        if hi > lc:
            parts.append(from_next_chunk(x_ref[max(lo, lc) - lc:hi - lc]))
        return parts[0] if len(parts) == 1 else jnp.concatenate(parts, axis=0)

    z = -lam
    softplus = jnp.maximum(z, 0.0) + jnp.log1p(jnp.exp(-jnp.abs(z)))
    half_l2 = (-0.5 * LRU_C * LOG2_E) * softplus
    half_ba = 0.5 * ba
    half_bx = 0.5 * bx

    for t0 in range(0, lc, ct):
        xc = cb[None, None]
        for k in range(CONV_W):
            xc = xc + shifted(t0, k - CONV_LEFT, ct) * cw[k:k + 1, :][None, None]
        xc2 = xc.reshape(ct * flat, hd)
        half_xc = 0.5 * xc2
        g = _dot(xc2.astype(BF16), wg_ref[...])
        fourier_piece(t0 // ct)
        for d in range(2):
            t_r = jnp.tanh(g[:, 2 * d * hd:(2 * d + 1) * hd] + half_ba[d:d + 1])
            t_i = jnp.tanh(g[:, (2 * d + 1) * hd:(2 * d + 2) * hd] + half_bx[d:d + 1])
            a = jnp.exp2(half_l2[d:d + 1] * t_r + half_l2[d:d + 1])
            m = 1.0 - a * a
            mult = jnp.where(m > 0.0, m * lax.rsqrt(m), 0.0)
            a_s[d, t0:t0 + ct] = a.reshape(ct, *tile)
            b_s[d, t0:t0 + ct] = ((mult * half_xc) * (t_i + 1.0)).reshape(ct, *tile)

    def scan_body(s, carry):
        hf, pf, hb, pb = carry
        tb = lc - 1 - s
        af, bf = a_s[0, s], b_s[0, s]
        ab, bb = a_s[1, tb], b_s[1, tb]
        hf = af * hf + bf
        pf = af * pf
        hb = ab * hb + bb
        pb = ab * pb
        h_s[0, s] = hf
        p_s[0, s] = pf
        h_s[1, tb] = hb
        p_s[1, tb] = pb
        return hf, pf, hb, pb

    zero, one = jnp.zeros(tile, F32), jnp.ones(tile, F32)
    hf, pf, hb, pb = lax.fori_loop(0, lc, scan_body, (zero, one, zero, one), unroll=8)

    def chain(h_end, p_end, h0, forward):
        shift, edge = (1, 0) if forward else (flat - 1, SUBLANES - 1)
        state = h0
        for _ in range(SUBLANES - 1):
            nxt = pltpu.roll((p_end * state + h_end).reshape(flat, hd), shift, 0).reshape(tile)
            state = jnp.where(sub == edge, h0, nxt)
        return state

    sf = chain(hf, pf, h0_ref[0], True)
    sb = chain(hb, pb, h0_ref[1], False)
    st_ref[0] = pf * sf + hf
    st_ref[1] = pb * sb + hb

    for t0 in range(0, lc, ct):
        sl = slice(t0, t0 + ct)
        out_ref[sl] = (h_s[0, sl] + p_s[0, sl] * sf[None]) + (h_s[1, sl] + p_s[1, sl] * sb[None])


def _seq_mix_call(l, lead_block, lc, nb, xr4, uv4, head_par, wg, h0, cmat, smat, ct, name, casts=()):
    n_tiles, n_seq, _, d_rnn = xr4.shape
    hd = d_rnn // N_LRU_HEADS
    assert N_LRU_HEADS == 2 * (uv4.shape[-1] // (2 * MXU_DIM))
    seq_len = lc * SUBLANES
    blk = (lc, nb, SUBLANES, hd)
    n_steps = n_seq // nb * N_LRU_HEADS
    step = lambda b, h: b * N_LRU_HEADS + h
    cast_in_specs, cast_out_specs, cast_out_shapes = [], [], []
    for w, src_layer in casts:
        k, n = w.shape[1:]
        assert k % (n_steps * 2 * SUBLANES) == 0
        cast_in_specs.append(pl.BlockSpec((None, k // n_steps, n), lambda b, h, j=src_layer: (j, step(b, h), 0)))
        cast_out_specs.append(pl.BlockSpec((k // n_steps, n), lambda b, h: (step(b, h), 0)))
        cast_out_shapes.append(jax.ShapeDtypeStruct((k, n), BF16))
    in_specs = [
        pl.BlockSpec(blk, lambda b, h: (lead_block, b, 0, h)),
        _resident((None,) + head_par.shape[1:], lambda b, h: (l, 0, 0, 0)),
        pl.BlockSpec((None, None) + wg.shape[2:], lambda b, h: (l, h, 0, 0)),
        pl.BlockSpec((None, 2, nb, SUBLANES, hd), lambda b, h: (l, 0, b, 0, h)),
        pl.BlockSpec((lc, nb, SUBLANES, 2 * MXU_DIM), lambda b, h: (lead_block, b, 0, h // 2)),
        _resident((seq_len, seq_len), lambda b, h: (0, 0)),
        _resident((seq_len, seq_len), lambda b, h: (0, 0)),
    ] + cast_in_specs
    return pl.pallas_call(
        functools.partial(_seq_mix_kernel, ct, len(casts)),
        grid=(n_seq // nb, N_LRU_HEADS),
        in_specs=in_specs,
        out_specs=[
            pl.BlockSpec(blk, lambda b, h: (lead_block, b, 0, h)),
            pl.BlockSpec((2, nb, SUBLANES, hd), lambda b, h: (0, b, 0, h)),
            pl.BlockSpec((lc // 2, nb, SUBLANES, MXU_DIM), lambda b, h: (h % 2, b, 0, h // 2)),
        ] + cast_out_specs,
        out_shape=[
            jax.ShapeDtypeStruct(xr4.shape, F32),
            jax.ShapeDtypeStruct((2, n_seq, SUBLANES, d_rnn), F32),
            jax.ShapeDtypeStruct((lc, n_seq, SUBLANES, uv4.shape[-1] // 2), F32),
        ] + cast_out_shapes,
        scratch_shapes=[
            pltpu.VMEM((2, lc, nb, SUBLANES, hd), F32),
            pltpu.VMEM((2, lc, nb, SUBLANES, hd), F32),
            pltpu.VMEM((2, lc, nb, SUBLANES, hd), F32),
            pltpu.VMEM((2, lc, nb, SUBLANES, hd), F32),
        ],
        input_output_aliases={0: 0},
        compiler_params=_params(("arbitrary", "arbitrary")),
        name=name,
    )(xr4, head_par, wg, h0, uv4, cmat, smat, *[w for w, _ in casts])


def _mix_ffn_kernel(d_ff, lat_tiles, final, split_x, x_ref, *rest):
    xctx_ref = rest[0] if split_x else None
    (rec_ref, gy_ref, ylat_ref, yctx_ref, gate_ref, g1_ref, sh2_ref, sc2_ref, g2_ref, n2_ref,
     wl_ref, wf_ref, wo_ref, w1_ref, w2_ref) = rest[1 if split_x else 0:][:15]
    rest = rest[(1 if split_x else 0) + 15:]
    fin_ref = rest[0] if final else None
    outs, act_s = rest[1 if final else 0:-1], rest[-1]
    d = x_ref.shape[1]
    is_lat = pl.program_id(0) < lat_tiles
    out_a = _dot((rec_ref[...] * gy_ref[...]).astype(BF16), wl_ref[...])
    out_b = _dot(_group_tile(lat_tiles, ylat_ref, yctx_ref).astype(BF16), wf_ref[...])
    merged = gate_ref[:, 0:d] * out_a + gate_ref[:, d:2 * d] * out_b
    x1 = _gated_add(_group_tile(lat_tiles, x_ref, xctx_ref), g1_ref, _dot(merged.astype(BF16), wo_ref[...]))
    h2 = _modulate(_rmsnorm(x1, n2_ref), sc2_ref, sh2_ref).astype(BF16)
    for c0, c1 in _ffn_chunks(d_ff):
        u = _dot(h2, w1_ref[:, c0:c1])
        v = _dot(h2, w1_ref[:, d_ff + c0:d_ff + c1])
        act_s[:, c0:c1] = ((u * _sigmoid(u)) * v).astype(BF16)
    x2 = _gated_add(x1, g2_ref, _dot(act_s[...], w2_ref[...]))
    if fin_ref is None:
        outs[0][...] = x2
    else:
        xn = _rmsnorm(x2, fin_ref)

        @pl.when(is_lat)
        def _():
            outs[0][...] = xn

        @pl.when(jnp.logical_not(is_lat))
        def _():
            outs[1][...] = xn


def _ffn_chunks(d_ff):
    n_tiles = d_ff // MXU_DIM
    half = (n_tiles + 1) // 2 * MXU_DIM
    return [(0, half), (half, d_ff)]


def _mix_ffn_call(l, x, rec, gy, y_lat, y_ctx, gate, mod, norm_g, wl, wf, wo, w1, w2, tm, lat_tiles, final_g):
    xs = x if isinstance(x, tuple) else (x,)
    rows, d = sum(a.shape[0] for a in xs), xs[0].shape[1]
    d_rnn = rec.shape[1]
    d_fnet, d_ff = wf.shape[0], w2.shape[0]
    n_tiles = rows // tm
    grp = lambda i: jnp.where(i < lat_tiles, 0, 1)
    tile = lambda w: pl.BlockSpec((tm, w), lambda i: (i, 0))
    modspec = lambda j: pl.BlockSpec((None, MOD_ROWS, d), lambda i: (l, grp(i), j))
    in_specs = (_group_specs(tm, d, lat_tiles) if len(xs) == 2 else [tile(d)]) + [
        tile(d_rnn), tile(d_rnn), *_group_specs(tm, d_fnet, lat_tiles), tile(2 * d),
        modspec(2), modspec(3), modspec(4), modspec(5),
        pl.BlockSpec((None, 1, d), lambda i: (l, 0, 0)),
    ] + [_resident(w.shape, lambda i: (0, 0)) for w in (wl, wf, wo, w1, w2)]
    args = [*xs, rec, gy, y_lat, y_ctx, gate, mod, mod, mod, mod, norm_g, wl, wf, wo, w1, w2]
    if final_g is None:
        out_specs = tile(d)
        out_shape = jax.ShapeDtypeStruct((rows, d), F32)
    else:
        in_specs.append(pl.BlockSpec((1, d), lambda i: (0, 0)))
        args.append(final_g)
        out_specs = _group_specs(tm, d, lat_tiles)
        out_shape = [
            jax.ShapeDtypeStruct((lat_tiles * tm, d), F32),
            jax.ShapeDtypeStruct(((n_tiles - lat_tiles) * tm, d), F32),
        ]
    return pl.pallas_call(
        functools.partial(_mix_ffn_kernel, d_ff, lat_tiles, final_g is not None, len(xs) == 2),
        grid=(n_tiles,),
        in_specs=in_specs,
        out_specs=out_specs,
        out_shape=out_shape,
        scratch_shapes=[pltpu.VMEM((tm, d_ff), BF16)],
        compiler_params=_params(("arbitrary",)),
        name="mix_ffn",
    )(*args)


def _cos_sin(n):
    k = np.arange(n)
    ang = 2.0 * np.pi * ((k[:, None] * k[None, :]) % n) / n
    return np.cos(ang), np.sin(ang)


def _channel_dft(d_fnet):
    gd = d_fnet // N_FNET_GROUPS
    c, s = _cos_sin(gd)
    eye = np.eye(N_FNET_GROUPS)
    scale = 1.0 / np.sqrt(gd)
    u, v = np.kron(eye, c) * scale, np.kron(eye, s) * scale
    blocks = []
    for j in range(0, d_fnet, MXU_DIM):
        blocks += [u[:, j:j + MXU_DIM], v[:, j:j + MXU_DIM]]
    return np.concatenate(blocks, axis=1)


def _row_order(seq_len):
    j = np.arange(seq_len)
    return (j % SUBLANES) * (seq_len // SUBLANES) + j // SUBLANES


def _seq_dft(n):
    c, s = _cos_sin(n)
    scale = 1.0 / np.sqrt(n)
    p = _row_order(n)
    return (c * scale)[p][:, p], (s * scale)[p][:, p]


def _grid_dft(n_rows, n_cols):
    cr, sr = _cos_sin(n_rows)
    cw, sw = _cos_sin(n_cols)
    scale = 1.0 / np.sqrt(n_rows * n_cols)
    p = _row_order(n_rows * n_cols)
    c = (np.kron(cr, cw) - np.kron(sr, sw)) * scale
    s = (np.kron(sr, cw) + np.kron(cr, sw)) * scale
    return c[p][:, p], s[p][:, p]


def _to_chunk_layout(x):
    b, l, d = x.shape
    return x.reshape(b, SUBLANES, l // SUBLANES, d).transpose(2, 0, 1, 3).reshape(b * l, d)


def _from_chunk_layout(rows, b, l):
    d = rows.shape[-1]
    return rows.reshape(l // SUBLANES, b, SUBLANES, d).transpose(1, 2, 0, 3).reshape(b, l, d)


def kernel(x_prompt, x_sample, state_lru, c, c_ctx, norm1_g, norm2_g, ada_w, ada_b, w_in, b_in, conv_w, conv_b,
           lru_wa, lru_ba, lru_wx, lru_bx, lru_lambda, w_lru_out, w_fnet_out, w_out, ffn_w_in, ffn_w_out,
           final_g):
    n_ctx, l_ctx, d = x_prompt.shape
    n_lat, l_lat, _ = x_sample.shape
    depth = w_in.shape[0]
    d_rnn = conv_w.shape[-1]
    d_fnet = w_fnet_out.shape[1]
    n_in = w_in.shape[-1]
    assert MOD_ROWS % (n_lat * SUBLANES) == 0 and MOD_ROWS % (n_ctx * SUBLANES) == 0
    assert l_lat % GRID_W == 0 and n_in == 2 * d_rnn + d_fnet + 2 * d
    rows_lat, rows_ctx = n_lat * l_lat, n_ctx * l_ctx
    rows = rows_lat + rows_ctx
    assert rows_lat % rows_ctx == 0
    lc_lat, lc_ctx = l_lat // SUBLANES, l_ctx // SUBLANES
    ctx_block = rows_lat // rows_ctx

    tm_in, tm_ffn = 1024, 512
    tm_in_first = 512

    x = (_to_chunk_layout(x_sample), _to_chunk_layout(x_prompt))

    c_pat = jnp.concatenate([
        jnp.tile(jnp.repeat(c, SUBLANES, axis=0), (MOD_ROWS // (n_lat * SUBLANES), 1)),
        jnp.broadcast_to(c_ctx[None, :], (MOD_ROWS, d))], axis=0)
    mod = _ada_call(c_pat, ada_w, ada_b)

    w_in_b = w_in[0].astype(BF16)
    wg = (0.5 * jnp.concatenate([lru_wa[:, 0], lru_wx[:, 0], lru_wa[:, 1], lru_wx[:, 1]], axis=-1)).astype(BF16)
    n_par = CONV_W + 1 + 6
    head_par = jnp.concatenate([conv_w, conv_b[:, None, :], lru_ba, lru_bx, lru_lambda,
                                jnp.zeros((depth, 2 * SUBLANES - n_par, d_rnn), F32)], axis=1)
    head_par = head_par.reshape(depth, 2 * SUBLANES, N_LRU_HEADS, d_rnn // N_LRU_HEADS).transpose(0, 2, 1, 3)
    b_in3 = b_in.reshape(depth, 1, n_in)
    n1 = norm1_g.reshape(depth, 1, d)
    n2 = norm2_g.reshape(depth, 1, d)
    h0_lat = jnp.broadcast_to(state_lru.transpose(1, 2, 0, 3)[:, :, :, None, :],
                              (depth, 2, n_lat, SUBLANES, d_rnn))
    h0_ctx = jnp.zeros((depth, 2, n_ctx, SUBLANES, d_rnn), F32)

    wc = jnp.asarray(_channel_dft(d_fnet), F32).astype(BF16)
    c_ctx_m, s_ctx_m = (jnp.asarray(m, F32).astype(BF16) for m in _seq_dft(l_ctx))
    c_lat_m, s_lat_m = (jnp.asarray(m, F32).astype(BF16) for m in _grid_dft(l_lat // GRID_W, GRID_W))

    as_lat = lambda a: a.reshape(-1, n_lat, SUBLANES, a.shape[-1])
    as_ctx = lambda a: a.reshape(-1, n_ctx, SUBLANES, a.shape[-1])

    states = []
    for l in range(depth):
        tm = tm_in_first if l == 0 else tm_in
        xr, gy, uv, gate = _inproj_call(l, x, mod, n1, w_in_b, b_in3, wc, tm, rows_lat // tm, d_rnn, d_fnet)

        rec, _, y_lat, w1_b, wl_b, wo_b = _seq_mix_call(l, 0, lc_lat, 1, as_lat(xr), as_lat(uv), head_par, wg,
                                                        h0_lat, c_lat_m, s_lat_m, 64, "seq_mix_lat",
                                                        [(ffn_w_in, l), (w_lru_out, l), (w_out, l)])
        casts = [(w_fnet_out, l), (ffn_w_out, l)]
        if l + 1 < depth:
            casts.append((w_in, l + 1))
        rec, st, y_ctx, *w_b = _seq_mix_call(l, ctx_block, lc_ctx, 4, as_ctx(rec), as_ctx(uv), head_par, wg,
                                             h0_ctx, c_ctx_m, s_ctx_m, 16, "seq_mix_ctx", casts)
        states.append(jnp.stack([st[0, :, SUBLANES - 1], st[1, :, 0]], axis=1))

        x = _mix_ffn_call(l, x, rec.reshape(rows, d_rnn), gy, y_lat.reshape(rows_lat, d_fnet),
                          y_ctx.reshape(rows_ctx, d_fnet), gate, mod, n2,
                          wl_b, w_b[0], wo_b, w1_b, w_b[1], tm_ffn, rows_lat // tm_ffn,
                          final_g.reshape(1, d) if l == depth - 1 else None)
        if l + 1 < depth:
            w_in_b = w_b[2]

    y_sample = _from_chunk_layout(x[0], n_lat, l_lat)
    y_prompt = _from_chunk_layout(x[1], n_ctx, l_ctx)
    new_state = jnp.stack(states, axis=1)
    return (y_prompt, y_sample, new_state)
```

```python
import functools

import numpy as np
import jax
import jax.numpy as jnp
from jax import lax
from jax.experimental import pallas as pl
from jax.experimental.pallas import tpu as pltpu

F32 = jnp.float32
BF16 = jnp.bfloat16

GRID_W = 64
N_LRU_HEADS = 4
LRU_C = 8.0
CONV_W = 4
CONV_LEFT = 2
N_FNET_GROUPS = 8
EPS = 1e-6
LOG2_E = 1.4426950408889634

SUBLANES = 8
MXU_DIM = 256
MOD_ROWS = 128
VMEM_LIMIT = 60 * 1024 * 1024


def _sigmoid(x):
    return 0.5 * jnp.tanh(0.5 * x) + 0.5


def _dot(a, b):
    return jnp.dot(a, b, preferred_element_type=F32)


def _params(sem):
    return pltpu.CompilerParams(dimension_semantics=sem, vmem_limit_bytes=VMEM_LIMIT)


def _resident(block_shape, index_map):
    return pl.BlockSpec(block_shape, index_map, pipeline_mode=pl.Buffered(1))


def _modulate(y, scale_ref, shift_ref):
    rows, d = y.shape
    y3 = y.reshape(rows // MOD_ROWS, MOD_ROWS, d)
    y3 = y3 * (1.0 + scale_ref[...])[None] + shift_ref[...][None]
    return y3.reshape(rows, d)


def _gated_add(x, gate_ref, upd):
    rows, d = x.shape
    x3 = x.reshape(rows // MOD_ROWS, MOD_ROWS, d)
    u3 = upd.reshape(rows // MOD_ROWS, MOD_ROWS, d)
    return (x3 + gate_ref[...][None] * u3).reshape(rows, d)


def _rmsnorm(x, g_ref):
    ms = jnp.mean(x * x, axis=-1, keepdims=True)
    return x * lax.rsqrt(ms + EPS) * g_ref[...]


def _ada_kernel(c_ref, w_ref, b_ref, o_ref):
    @pl.when(pl.program_id(1) == 0)
    def _():
        o_ref[...] = jnp.broadcast_to(b_ref[...], o_ref.shape)

    c = c_ref[...]
    cs = c * _sigmoid(c)
    o_ref[...] += _dot(cs.astype(BF16), w_ref[...].astype(BF16))


def _ada_call(c_pat, ada_w, ada_b):
    depth, d, n6 = ada_w.shape
    rows = c_pat.shape[0]
    tk = MXU_DIM
    return pl.pallas_call(
        _ada_kernel,
        grid=(depth, d // tk),
        in_specs=[
            pl.BlockSpec((rows, tk), lambda l, k: (0, k)),
            pl.BlockSpec((None, tk, n6), lambda l, k: (l, k, 0)),
            pl.BlockSpec((None, 1, n6), lambda l, k: (l, 0, 0)),
        ],
        out_specs=pl.BlockSpec((None, rows, n6), lambda l, k: (l, 0, 0)),
        out_shape=jax.ShapeDtypeStruct((depth, rows, n6), F32),
        compiler_params=_params(("arbitrary", "arbitrary")),
        name="ada_mod",
    )(c_pat, ada_w, ada_b.reshape(depth, 1, n6))


def _group_tile(lat_tiles, lat_ref, ctx_ref):
    if ctx_ref is None:
        return lat_ref[...]
    return jnp.where(pl.program_id(0) < lat_tiles, lat_ref[...], ctx_ref[...])


def _group_specs(tm, width, lat_tiles):
    return [pl.BlockSpec((tm, width), lambda i: (jnp.minimum(i, lat_tiles - 1), 0)),
            pl.BlockSpec((tm, width), lambda i: (jnp.maximum(i - lat_tiles, 0), 0))]


def _inproj_kernel(d_rnn, d_fnet, lat_tiles, split_x, x_ref, *rest):
    xctx_ref = rest[0] if split_x else None
    sh_ref, sc_ref, g_ref, w_ref, b_ref, wc_ref, xr_ref, gy_ref, uv_ref, gate_ref = rest[1 if split_x else 0:]
    h = _modulate(_rmsnorm(_group_tile(lat_tiles, x_ref, xctx_ref), g_ref), sc_ref, sh_ref).astype(BF16)
    o1, o2, o3 = d_rnn, 2 * d_rnn, 2 * d_rnn + d_fnet
    n_in = w_ref.shape[1]
    gate_ref[...] = _sigmoid(_dot(h, w_ref[:, o3:n_in]) + b_ref[:, o3:n_in]).astype(BF16)
    gy_ref[...] = jax.nn.gelu(_dot(h, w_ref[:, o1:o2]) + b_ref[:, o1:o2]).astype(BF16)
    xf = _dot(h, w_ref[:, o2:o3]) + b_ref[:, o2:o3]
    uv_ref[...] = _dot(xf.astype(BF16), wc_ref[...])
    xr_ref[...] = _dot(h, w_ref[:, 0:o1]) + b_ref[:, 0:o1]


def _inproj_call(l, x, mod, norm_g, w_in, b_in, wc, tm, lat_tiles, d_rnn, d_fnet):
    xs = x if isinstance(x, tuple) else (x,)
    rows, d = sum(a.shape[0] for a in xs), xs[0].shape[1]
    n_in = w_in.shape[1]
    n_gate = n_in - 2 * d_rnn - d_fnet
    grp = lambda i: jnp.where(i < lat_tiles, 0, 1)
    x_specs = _group_specs(tm, d, lat_tiles) if len(xs) == 2 else [pl.BlockSpec((tm, d), lambda i: (i, 0))]
    return pl.pallas_call(
        functools.partial(_inproj_kernel, d_rnn, d_fnet, lat_tiles, len(xs) == 2),
        grid=(rows // tm,),
        in_specs=x_specs + [
            pl.BlockSpec((None, MOD_ROWS, d), lambda i: (l, grp(i), 0)),
            pl.BlockSpec((None, MOD_ROWS, d), lambda i: (l, grp(i), 1)),
            pl.BlockSpec((None, 1, d), lambda i: (l, 0, 0)),
            _resident((d, n_in), lambda i: (0, 0)),
            pl.BlockSpec((None, 1, n_in), lambda i: (l, 0, 0)),
            _resident((d_fnet, 2 * d_fnet), lambda i: (0, 0)),
        ],
        out_specs=[
            pl.BlockSpec((tm, d_rnn), lambda i: (i, 0)),
            pl.BlockSpec((tm, d_rnn), lambda i: (i, 0)),
            pl.BlockSpec((tm, 2 * d_fnet), lambda i: (i, 0)),
            pl.BlockSpec((tm, n_gate), lambda i: (i, 0)),
        ],
        out_shape=[
            jax.ShapeDtypeStruct((rows, d_rnn), F32),
            jax.ShapeDtypeStruct((rows, d_rnn), BF16),
            jax.ShapeDtypeStruct((rows, 2 * d_fnet), F32),
            jax.ShapeDtypeStruct((rows, n_gate), BF16),
        ],
        compiler_params=_params(("arbitrary",)),
        name="in_proj",
    )(*xs, mod, mod, norm_g, w_in, b_in, wc)


def _seq_mix_kernel(ct, n_cast, x_ref, cw_ref, cb_ref, wg_ref, ba_ref, bx_ref, lam_ref, h0_ref,
                    uv_ref, c_ref, s_ref, *rest):
    cast_in, (out_ref, st_ref, y_ref), cast_out = rest[:n_cast], rest[n_cast:n_cast + 3], rest[n_cast + 3:-4]
    a_s, b_s, h_s, p_s = rest[-4:]
    for w_ref, wb_ref in zip(cast_in, cast_out):
        wb_ref[...] = w_ref[...].astype(BF16)
    lc, nb, _, hd = x_ref.shape
    tile = (nb, SUBLANES, hd)
    flat = nb * SUBLANES
    sub = lax.broadcasted_iota(jnp.int32, tile, 1)

    width = y_ref.shape[-1]
    seq_rows = lc * SUBLANES
    n_chunks = lc // ct
    piece = seq_rows // 2 // n_chunks
    side_by_side = lambda lo: jnp.concatenate(
        [uv_ref[:, s, :, lo:lo + width].reshape(seq_rows, width) for s in range(nb)], axis=1).astype(BF16)
    u_all, v_all = side_by_side(0), side_by_side(width)

    def fourier_piece(k):
        r0 = pl.multiple_of((pl.program_id(1) % 2) * (seq_rows // 2) + k * piece, piece)
        y = _dot(c_ref[pl.ds(r0, piece), :], u_all) - _dot(s_ref[pl.ds(r0, piece), :], v_all)
        tiles = slice(k * piece // SUBLANES, (k + 1) * piece // SUBLANES)
        for s in range(nb):
            y_ref[tiles, s] = y[:, s * width:(s + 1) * width].reshape(piece // SUBLANES, SUBLANES, width)

    def from_prev_chunk(v):
        n = v.shape[0] * flat
        r = pltpu.roll(v.reshape(n, hd), 1, 0).reshape(v.shape)
        return jnp.where(sub[None] == 0, 0.0, r)

    def from_next_chunk(v):
        n = v.shape[0] * flat
        r = pltpu.roll(v.reshape(n, hd), n - 1, 0).reshape(v.shape)
        return jnp.where(sub[None] == SUBLANES - 1, 0.0, r)

    def shifted(t0, off, n):
        lo, hi = t0 + off, t0 + off + n
        parts = []
        if lo < 0:
            parts.append(from_prev_chunk(x_ref[lc + lo:lc + min(hi, 0)]))
        if hi > 0 and lo < lc:
            parts.append(x_ref[max(lo, 0):min(hi, lc)])
        if hi > lc:
            parts.append(from_next_chunk(x_ref[max(lo, lc) - lc:hi - lc]))
        return parts[0] if len(parts) == 1 else jnp.concatenate(parts, axis=0)

    z = -lam_ref[...]
    softplus = jnp.maximum(z, 0.0) + jnp.log1p(jnp.exp(-jnp.abs(z)))
    half_l2 = (-0.5 * LRU_C * LOG2_E) * softplus
    half_ba = 0.5 * ba_ref[...]
    half_bx = 0.5 * bx_ref[...]

    for t0 in range(0, lc, ct):
        xc = cb_ref[...][None, None]
        for k in range(CONV_W):
            xc = xc + shifted(t0, k - CONV_LEFT, ct) * cw_ref[k:k + 1, :][None, None]
        xc2 = xc.reshape(ct * flat, hd)
        half_xc = 0.5 * xc2
        g = _dot(xc2.astype(BF16), wg_ref[...])
        fourier_piece(t0 // ct)
        for d in range(2):
            t_r = jnp.tanh(g[:, 2 * d * hd:(2 * d + 1) * hd] + half_ba[d])
            t_i = jnp.tanh(g[:, (2 * d + 1) * hd:(2 * d + 2) * hd] + half_bx[d])
            a = jnp.exp2(half_l2[d] * t_r + half_l2[d])
            m = 1.0 - a * a
            mult = jnp.where(m > 0.0, m * lax.rsqrt(m), 0.0)
            a_s[d, t0:t0 + ct] = a.reshape(ct, *tile)
            b_s[d, t0:t0 + ct] = ((mult * half_xc) * (t_i + 1.0)).reshape(ct, *tile)

    def scan_body(s, carry):
        hf, pf, hb, pb = carry
        tb = lc - 1 - s
        af, bf = a_s[0, s], b_s[0, s]
        ab, bb = a_s[1, tb], b_s[1, tb]
        hf = af * hf + bf
        pf = af * pf
        hb = ab * hb + bb
        pb = ab * pb
        h_s[0, s] = hf
        p_s[0, s] = pf
        h_s[1, tb] = hb
        p_s[1, tb] = pb
        return hf, pf, hb, pb

    zero, one = jnp.zeros(tile, F32), jnp.ones(tile, F32)
    hf, pf, hb, pb = lax.fori_loop(0, lc, scan_body, (zero, one, zero, one), unroll=8)

    def chain(h_end, p_end, h0, forward):
        shift, edge = (1, 0) if forward else (flat - 1, SUBLANES - 1)
        state = h0
        for _ in range(SUBLANES - 1):
            nxt = pltpu.roll((p_end * state + h_end).reshape(flat, hd), shift, 0).reshape(tile)
            state = jnp.where(sub == edge, h0, nxt)
        return state

    sf = chain(hf, pf, h0_ref[0], True)
    sb = chain(hb, pb, h0_ref[1], False)
    st_ref[0] = pf * sf + hf
    st_ref[1] = pb * sb + hb

    for t0 in range(0, lc, ct):
        sl = slice(t0, t0 + ct)
        out_ref[sl] = (h_s[0, sl] + p_s[0, sl] * sf[None]) + (h_s[1, sl] + p_s[1, sl] * sb[None])


def _seq_mix_call(l, lead_block, lc, nb, xr4, uv4, conv_w, conv_b, wg, ba, bx, lam, h0, cmat, smat, ct, name,
                  casts=()):
    n_tiles, n_seq, _, d_rnn = xr4.shape
    hd = d_rnn // N_LRU_HEADS
    assert N_LRU_HEADS == 2 * (uv4.shape[-1] // (2 * MXU_DIM))
    seq_len = lc * SUBLANES
    blk = (lc, nb, SUBLANES, hd)
    n_steps = n_seq // nb * N_LRU_HEADS
    step = lambda b, h: b * N_LRU_HEADS + h
    cast_in_specs, cast_out_specs, cast_out_shapes = [], [], []
    for w, src_layer in casts:
        k, n = w.shape[1:]
        assert k % (n_steps * 2 * SUBLANES) == 0
        cast_in_specs.append(pl.BlockSpec((None, k // n_steps, n), lambda b, h, j=src_layer: (j, step(b, h), 0)))
        cast_out_specs.append(pl.BlockSpec((k // n_steps, n), lambda b, h: (step(b, h), 0)))
        cast_out_shapes.append(jax.ShapeDtypeStruct((k, n), BF16))
    in_specs = [
        pl.BlockSpec(blk, lambda b, h: (lead_block, b, 0, h)),
        pl.BlockSpec((None, CONV_W, hd), lambda b, h: (l, 0, h)),
        pl.BlockSpec((None, 1, hd), lambda b, h: (l, 0, h)),
        pl.BlockSpec((None, None, hd, 4 * hd), lambda b, h: (l, h, 0, 0)),
        pl.BlockSpec((None, 2, 1, hd), lambda b, h: (l, 0, 0, h)),
        pl.BlockSpec((None, 2, 1, hd), lambda b, h: (l, 0, 0, h)),
        pl.BlockSpec((None, 2, 1, hd), lambda b, h: (l, 0, 0, h)),
        pl.BlockSpec((None, 2, nb, SUBLANES, hd), lambda b, h: (l, 0, b, 0, h)),
        pl.BlockSpec((lc, nb, SUBLANES, 2 * MXU_DIM), lambda b, h: (lead_block, b, 0, h // 2)),
        _resident((seq_len, seq_len), lambda b, h: (0, 0)),
        _resident((seq_len, seq_len), lambda b, h: (0, 0)),
    ] + cast_in_specs
    return pl.pallas_call(
        functools.partial(_seq_mix_kernel, ct, len(casts)),
        grid=(n_seq // nb, N_LRU_HEADS),
        in_specs=in_specs,
        out_specs=[
            pl.BlockSpec(blk, lambda b, h: (lead_block, b, 0, h)),
            pl.BlockSpec((2, nb, SUBLANES, hd), lambda b, h: (0, b, 0, h)),
            pl.BlockSpec((lc // 2, nb, SUBLANES, MXU_DIM), lambda b, h: (h % 2, b, 0, h // 2)),
        ] + cast_out_specs,
        out_shape=[
            jax.ShapeDtypeStruct(xr4.shape, F32),
            jax.ShapeDtypeStruct((2, n_seq, SUBLANES, d_rnn), F32),
            jax.ShapeDtypeStruct((lc, n_seq, SUBLANES, uv4.shape[-1] // 2), F32),
        ] + cast_out_shapes,
        scratch_shapes=[
            pltpu.VMEM((2, lc, nb, SUBLANES, hd), F32),
            pltpu.VMEM((2, lc, nb, SUBLANES, hd), F32),
            pltpu.VMEM((2, lc, nb, SUBLANES, hd), F32),
            pltpu.VMEM((2, lc, nb, SUBLANES, hd), F32),
        ],
        input_output_aliases={0: 0},
        compiler_params=_params(("arbitrary", "arbitrary")),
        name=name,
    )(xr4, conv_w, conv_b, wg, ba, bx, lam, h0, uv4, cmat, smat, *[w for w, _ in casts])


def _mix_ffn_kernel(d_ff, lat_tiles, final, split_x, x_ref, *rest):
    xctx_ref = rest[0] if split_x else None
    (rec_ref, gy_ref, ylat_ref, yctx_ref, gate_ref, g1_ref, sh2_ref, sc2_ref, g2_ref, n2_ref,
     wl_ref, wf_ref, wo_ref, w1_ref, w2_ref) = rest[1 if split_x else 0:][:15]
    rest = rest[(1 if split_x else 0) + 15:]
    fin_ref = rest[0] if final else None
    outs, act_s = rest[1 if final else 0:-1], rest[-1]
    d = x_ref.shape[1]
    is_lat = pl.program_id(0) < lat_tiles
    out_a = _dot((rec_ref[...] * gy_ref[...]).astype(BF16), wl_ref[...])
    out_b = _dot(_group_tile(lat_tiles, ylat_ref, yctx_ref).astype(BF16), wf_ref[...])
    merged = gate_ref[:, 0:d] * out_a + gate_ref[:, d:2 * d] * out_b
    x1 = _gated_add(_group_tile(lat_tiles, x_ref, xctx_ref), g1_ref, _dot(merged.astype(BF16), wo_ref[...]))
    h2 = _modulate(_rmsnorm(x1, n2_ref), sc2_ref, sh2_ref).astype(BF16)
    for c0, c1 in _ffn_chunks(d_ff):
        u = _dot(h2, w1_ref[:, c0:c1])
        v = _dot(h2, w1_ref[:, d_ff + c0:d_ff + c1])
        act_s[:, c0:c1] = ((u * _sigmoid(u)) * v).astype(BF16)
    x2 = _gated_add(x1, g2_ref, _dot(act_s[...], w2_ref[...]))
    if fin_ref is None:
        outs[0][...] = x2
    else:
        xn = _rmsnorm(x2, fin_ref)

        @pl.when(is_lat)
        def _():
            outs[0][...] = xn

        @pl.when(jnp.logical_not(is_lat))
        def _():
            outs[1][...] = xn


def _ffn_chunks(d_ff):
    n_tiles = d_ff // MXU_DIM
    half = (n_tiles + 1) // 2 * MXU_DIM
    return [(0, half), (half, d_ff)]


def _mix_ffn_call(l, x, rec, gy, y_lat, y_ctx, gate, mod, norm_g, wl, wf, wo, w1, w2, tm, lat_tiles, final_g):
    xs = x if isinstance(x, tuple) else (x,)
    rows, d = sum(a.shape[0] for a in xs), xs[0].shape[1]
    d_rnn = rec.shape[1]
    d_fnet, d_ff = wf.shape[0], w2.shape[0]
    n_tiles = rows // tm
    grp = lambda i: jnp.where(i < lat_tiles, 0, 1)
    tile = lambda w: pl.BlockSpec((tm, w), lambda i: (i, 0))
    modspec = lambda j: pl.BlockSpec((None, MOD_ROWS, d), lambda i: (l, grp(i), j))
    in_specs = (_group_specs(tm, d, lat_tiles) if len(xs) == 2 else [tile(d)]) + [
        tile(d_rnn), tile(d_rnn), *_group_specs(tm, d_fnet, lat_tiles), tile(2 * d),
        modspec(2), modspec(3), modspec(4), modspec(5),
        pl.BlockSpec((None, 1, d), lambda i: (l, 0, 0)),
    ] + [_resident(w.shape, lambda i: (0, 0)) for w in (wl, wf, wo, w1, w2)]
    args = [*xs, rec, gy, y_lat, y_ctx, gate, mod, mod, mod, mod, norm_g, wl, wf, wo, w1, w2]
    if final_g is None:
        out_specs = tile(d)
        out_shape = jax.ShapeDtypeStruct((rows, d), F32)
    else:
        in_specs.append(pl.BlockSpec((1, d), lambda i: (0, 0)))
        args.append(final_g)
        out_specs = _group_specs(tm, d, lat_tiles)
        out_shape = [
            jax.ShapeDtypeStruct((lat_tiles * tm, d), F32),
            jax.ShapeDtypeStruct(((n_tiles - lat_tiles) * tm, d), F32),
        ]
    return pl.pallas_call(
        functools.partial(_mix_ffn_kernel, d_ff, lat_tiles, final_g is not None, len(xs) == 2),
        grid=(n_tiles,),
        in_specs=in_specs,
        out_specs=out_specs,
        out_shape=out_shape,
        scratch_shapes=[pltpu.VMEM((tm, d_ff), BF16)],
        compiler_params=_params(("arbitrary",)),
        name="mix_ffn",
    )(*args)


def _cos_sin(n):
    k = np.arange(n)
    ang = 2.0 * np.pi * ((k[:, None] * k[None, :]) % n) / n
    return np.cos(ang), np.sin(ang)


def _channel_dft(d_fnet):
    gd = d_fnet // N_FNET_GROUPS
    c, s = _cos_sin(gd)
    eye = np.eye(N_FNET_GROUPS)
    scale = 1.0 / np.sqrt(gd)
    u, v = np.kron(eye, c) * scale, np.kron(eye, s) * scale
    blocks = []
    for j in range(0, d_fnet, MXU_DIM):
        blocks += [u[:, j:j + MXU_DIM], v[:, j:j + MXU_DIM]]
    return np.concatenate(blocks, axis=1)


def _row_order(seq_len):
    j = np.arange(seq_len)
    return (j % SUBLANES) * (seq_len // SUBLANES) + j // SUBLANES


def _seq_dft(n):
    c, s = _cos_sin(n)
    scale = 1.0 / np.sqrt(n)
    p = _row_order(n)
    return (c * scale)[p][:, p], (s * scale)[p][:, p]


def _grid_dft(n_rows, n_cols):
    cr, sr = _cos_sin(n_rows)
    cw, sw = _cos_sin(n_cols)
    scale = 1.0 / np.sqrt(n_rows * n_cols)
    p = _row_order(n_rows * n_cols)
    c = (np.kron(cr, cw) - np.kron(sr, sw)) * scale
    s = (np.kron(sr, cw) + np.kron(cr, sw)) * scale
    return c[p][:, p], s[p][:, p]


def _to_chunk_layout(x):
    b, l, d = x.shape
    return x.reshape(b, SUBLANES, l // SUBLANES, d).transpose(2, 0, 1, 3).reshape(b * l, d)


def _from_chunk_layout(rows, b, l):
    d = rows.shape[-1]
    return rows.reshape(l // SUBLANES, b, SUBLANES, d).transpose(1, 2, 0, 3).reshape(b, l, d)


def kernel(x_prompt, x_sample, state_lru, c, c_ctx, norm1_g, norm2_g, ada_w, ada_b, w_in, b_in, conv_w, conv_b,
           lru_wa, lru_ba, lru_wx, lru_bx, lru_lambda, w_lru_out, w_fnet_out, w_out, ffn_w_in, ffn_w_out,
           final_g):
    n_ctx, l_ctx, d = x_prompt.shape
    n_lat, l_lat, _ = x_sample.shape
    depth = w_in.shape[0]
    d_rnn = conv_w.shape[-1]
    d_fnet = w_fnet_out.shape[1]
    n_in = w_in.shape[-1]
    assert MOD_ROWS % (n_lat * SUBLANES) == 0 and MOD_ROWS % (n_ctx * SUBLANES) == 0
    assert l_lat % GRID_W == 0 and n_in == 2 * d_rnn + d_fnet + 2 * d
    rows_lat, rows_ctx = n_lat * l_lat, n_ctx * l_ctx
    rows = rows_lat + rows_ctx
    assert rows_lat % rows_ctx == 0
    lc_lat, lc_ctx = l_lat // SUBLANES, l_ctx // SUBLANES
    ctx_block = rows_lat // rows_ctx

    tm_in, tm_ffn = 1024, 512
    tm_in_first = 512

    x = (_to_chunk_layout(x_sample), _to_chunk_layout(x_prompt))

    c_pat = jnp.concatenate([
        jnp.tile(jnp.repeat(c, SUBLANES, axis=0), (MOD_ROWS // (n_lat * SUBLANES), 1)),
        jnp.broadcast_to(c_ctx[None, :], (MOD_ROWS, d))], axis=0)
    mod = _ada_call(c_pat, ada_w, ada_b)

    w_in_b = w_in[0].astype(BF16)
    wg = (0.5 * jnp.concatenate([lru_wa[:, 0], lru_wx[:, 0], lru_wa[:, 1], lru_wx[:, 1]], axis=-1)).astype(BF16)
    b_in3 = b_in.reshape(depth, 1, n_in)
    n1 = norm1_g.reshape(depth, 1, d)
    n2 = norm2_g.reshape(depth, 1, d)
    conv_b3 = conv_b.reshape(depth, 1, d_rnn)
    ba4 = lru_ba.reshape(depth, 2, 1, d_rnn)
    bx4 = lru_bx.reshape(depth, 2, 1, d_rnn)
    lam4 = lru_lambda.reshape(depth, 2, 1, d_rnn)
    h0_lat = jnp.broadcast_to(state_lru.transpose(1, 2, 0, 3)[:, :, :, None, :],
                              (depth, 2, n_lat, SUBLANES, d_rnn))
    h0_ctx = jnp.zeros((depth, 2, n_ctx, SUBLANES, d_rnn), F32)

    wc = jnp.asarray(_channel_dft(d_fnet), F32).astype(BF16)
    c_ctx_m, s_ctx_m = (jnp.asarray(m, F32).astype(BF16) for m in _seq_dft(l_ctx))
    c_lat_m, s_lat_m = (jnp.asarray(m, F32).astype(BF16) for m in _grid_dft(l_lat // GRID_W, GRID_W))

    as_lat = lambda a: a.reshape(-1, n_lat, SUBLANES, a.shape[-1])
    as_ctx = lambda a: a.reshape(-1, n_ctx, SUBLANES, a.shape[-1])

    states = []
    for l in range(depth):
        tm = tm_in_first if l == 0 else tm_in
        xr, gy, uv, gate = _inproj_call(l, x, mod, n1, w_in_b, b_in3, wc, tm, rows_lat // tm, d_rnn, d_fnet)

        rec, _, y_lat, w1_b, wl_b, wo_b = _seq_mix_call(l, 0, lc_lat, 1, as_lat(xr), as_lat(uv), conv_w, conv_b3, wg,
                                                        ba4, bx4, lam4, h0_lat, c_lat_m, s_lat_m, 64, "seq_mix_lat",
                                                        [(ffn_w_in, l), (w_lru_out, l), (w_out, l)])
        casts = [(w_fnet_out, l), (ffn_w_out, l)]
        if l + 1 < depth:
            casts.append((w_in, l + 1))
        rec, st, y_ctx, *w_b = _seq_mix_call(l, ctx_block, lc_ctx, 4, as_ctx(rec), as_ctx(uv), conv_w, conv_b3, wg,
                                             ba4, bx4, lam4, h0_ctx, c_ctx_m, s_ctx_m, 16, "seq_mix_ctx", casts)
        states.append(jnp.stack([st[0, :, SUBLANES - 1], st[1, :, 0]], axis=1))

        x = _mix_ffn_call(l, x, rec.reshape(rows, d_rnn), gy, y_lat.reshape(rows_lat, d_fnet),
                          y_ctx.reshape(rows_ctx, d_fnet), gate, mod, n2,
                          wl_b, w_b[0], wo_b, w1_b, w_b[1], tm_ffn, rows_lat // tm_ffn,
                          final_g.reshape(1, d) if l == depth - 1 else None)
        if l + 1 < depth:
            w_in_b = w_b[2]

    y_sample = _from_chunk_layout(x[0], n_lat, l_lat)
    y_prompt = _from_chunk_layout(x[1], n_ctx, l_ctx)
    new_state = jnp.stack(states, axis=1)
    return (y_prompt, y_sample, new_state)
```

```python
import functools

import numpy as np
import jax
import jax.numpy as jnp
from jax import lax
from jax.experimental import pallas as pl
from jax.experimental.pallas import tpu as pltpu

F32 = jnp.float32
BF16 = jnp.bfloat16

GRID_W = 64
N_LRU_HEADS = 4
LRU_C = 8.0
CONV_W = 4
CONV_LEFT = 2
N_FNET_GROUPS = 8
EPS = 1e-6
LOG2_E = 1.4426950408889634

SUBLANES = 8
MXU_DIM = 256
MOD_ROWS = 128
VMEM_LIMIT = 60 * 1024 * 1024


def _sigmoid(x):
    return 0.5 * jnp.tanh(0.5 * x) + 0.5


def _dot(a, b):
    return jnp.dot(a, b, preferred_element_type=F32)


def _params(sem):
    return pltpu.CompilerParams(dimension_semantics=sem, vmem_limit_bytes=VMEM_LIMIT)


def _resident(block_shape, index_map):
    return pl.BlockSpec(block_shape, index_map, pipeline_mode=pl.Buffered(1))


def _modulate(y, scale_ref, shift_ref):
    rows, d = y.shape
    y3 = y.reshape(rows // MOD_ROWS, MOD_ROWS, d)
    y3 = y3 * (1.0 + scale_ref[...])[None] + shift_ref[...][None]
    return y3.reshape(rows, d)


def _gated_add(x, gate_ref, upd):
    rows, d = x.shape
    x3 = x.reshape(rows // MOD_ROWS, MOD_ROWS, d)
    u3 = upd.reshape(rows // MOD_ROWS, MOD_ROWS, d)
    return (x3 + gate_ref[...][None] * u3).reshape(rows, d)


def _rmsnorm(x, g_ref):
    ms = jnp.mean(x * x, axis=-1, keepdims=True)
    return x * lax.rsqrt(ms + EPS) * g_ref[...]


def _ada_kernel(c_ref, w_ref, b_ref, o_ref):
    @pl.when(pl.program_id(1) == 0)
    def _():
        o_ref[...] = jnp.broadcast_to(b_ref[...], o_ref.shape)

    c = c_ref[...]
    cs = c * _sigmoid(c)
    o_ref[...] += _dot(cs.astype(BF16), w_ref[...].astype(BF16))


def _ada_call(c_pat, ada_w, ada_b, first, count):
    depth, d, n6 = ada_w.shape
    rows = c_pat.shape[0]
    tk = MXU_DIM
    return pl.pallas_call(
        _ada_kernel,
        grid=(count, d // tk),
        in_specs=[
            pl.BlockSpec((rows, tk), lambda l, k: (0, k)),
            pl.BlockSpec((None, tk, n6), lambda l, k: (first + l, k, 0)),
            pl.BlockSpec((None, 1, n6), lambda l, k: (first + l, 0, 0)),
        ],
        out_specs=pl.BlockSpec((None, rows, n6), lambda l, k: (l, 0, 0)),
        out_shape=jax.ShapeDtypeStruct((count, rows, n6), F32),
        compiler_params=_params(("arbitrary", "arbitrary")),
        name="ada_mod",
    )(c_pat, ada_w, ada_b.reshape(depth, 1, n6))


def _group_tile(lat_tiles, lat_ref, ctx_ref):
    if ctx_ref is None:
        return lat_ref[...]
    return jnp.where(pl.program_id(0) < lat_tiles, lat_ref[...], ctx_ref[...])


def _group_specs(tm, width, lat_tiles):
    return [pl.BlockSpec((tm, width), lambda i: (jnp.minimum(i, lat_tiles - 1), 0)),
            pl.BlockSpec((tm, width), lambda i: (jnp.maximum(i - lat_tiles, 0), 0))]


def _inproj_kernel(d_rnn, d_fnet, lat_tiles, split_x, x_ref, *rest):
    xctx_ref = rest[0] if split_x else None
    sh_ref, sc_ref, g_ref, w_ref, b_ref, wc_ref, xr_ref, gy_ref, uv_ref, gate_ref = rest[1 if split_x else 0:]
    h = _modulate(_rmsnorm(_group_tile(lat_tiles, x_ref, xctx_ref), g_ref), sc_ref, sh_ref).astype(BF16)
    o1, o2, o3 = d_rnn, 2 * d_rnn, 2 * d_rnn + d_fnet
    n_in = w_ref.shape[1]
    gate_ref[...] = _sigmoid(_dot(h, w_ref[:, o3:n_in]) + b_ref[:, o3:n_in]).astype(BF16)
    gy_ref[...] = jax.nn.gelu(_dot(h, w_ref[:, o1:o2]) + b_ref[:, o1:o2]).astype(BF16)
    xf = _dot(h, w_ref[:, o2:o3]) + b_ref[:, o2:o3]
    uv_ref[...] = _dot(xf.astype(BF16), wc_ref[...])
    xr_ref[...] = _dot(h, w_ref[:, 0:o1]) + b_ref[:, 0:o1]


def _inproj_call(l, x, mod, norm_g, w_in, b_in, wc, tm, lat_tiles, d_rnn, d_fnet):
    xs = x if isinstance(x, tuple) else (x,)
    mod, mod_l = mod
    rows, d = sum(a.shape[0] for a in xs), xs[0].shape[1]
    n_in = w_in.shape[1]
    n_gate = n_in - 2 * d_rnn - d_fnet
    grp = lambda i: jnp.where(i < lat_tiles, 0, 1)
    x_specs = _group_specs(tm, d, lat_tiles) if len(xs) == 2 else [pl.BlockSpec((tm, d), lambda i: (i, 0))]
    return pl.pallas_call(
        functools.partial(_inproj_kernel, d_rnn, d_fnet, lat_tiles, len(xs) == 2),
        grid=(rows // tm,),
        in_specs=x_specs + [
            pl.BlockSpec((None, MOD_ROWS, d), lambda i: (mod_l, grp(i), 0)),
            pl.BlockSpec((None, MOD_ROWS, d), lambda i: (mod_l, grp(i), 1)),
            pl.BlockSpec((None, 1, d), lambda i: (l, 0, 0)),
            _resident((d, n_in), lambda i: (0, 0)),
            pl.BlockSpec((None, 1, n_in), lambda i: (l, 0, 0)),
            _resident((d_fnet, 2 * d_fnet), lambda i: (0, 0)),
        ],
        out_specs=[
            pl.BlockSpec((tm, d_rnn), lambda i: (i, 0)),
            pl.BlockSpec((tm, d_rnn), lambda i: (i, 0)),
            pl.BlockSpec((tm, 2 * d_fnet), lambda i: (i, 0)),
            pl.BlockSpec((tm, n_gate), lambda i: (i, 0)),
        ],
        out_shape=[
            jax.ShapeDtypeStruct((rows, d_rnn), F32),
            jax.ShapeDtypeStruct((rows, d_rnn), BF16),
            jax.ShapeDtypeStruct((rows, 2 * d_fnet), F32),
            jax.ShapeDtypeStruct((rows, n_gate), BF16),
        ],
        compiler_params=_params(("arbitrary",)),
        name="in_proj",
    )(*xs, mod, mod, norm_g, w_in, b_in, wc)


def _seq_mix_kernel(ct, n_cast, x_ref, cw_ref, cb_ref, wg_ref, ba_ref, bx_ref, lam_ref, h0_ref,
                    uv_ref, c_ref, s_ref, *rest):
    cast_in, (out_ref, st_ref, y_ref), cast_out = rest[:n_cast], rest[n_cast:n_cast + 3], rest[n_cast + 3:-4]
    a_s, b_s, h_s, p_s = rest[-4:]
    for w_ref, wb_ref in zip(cast_in, cast_out):
        wb_ref[...] = w_ref[...].astype(BF16)
    lc, nb, _, hd = x_ref.shape
    tile = (nb, SUBLANES, hd)
    flat = nb * SUBLANES
    sub = lax.broadcasted_iota(jnp.int32, tile, 1)

    width = y_ref.shape[-1]
    seq_rows = lc * SUBLANES
    n_chunks = lc // ct
    piece = seq_rows // 2 // n_chunks
    side_by_side = lambda lo: jnp.concatenate(
        [uv_ref[:, s, :, lo:lo + width].reshape(seq_rows, width) for s in range(nb)], axis=1).astype(BF16)
    u_all, v_all = side_by_side(0), side_by_side(width)

    def fourier_piece(k):
        r0 = pl.multiple_of((pl.program_id(1) % 2) * (seq_rows // 2) + k * piece, piece)
        y = _dot(c_ref[pl.ds(r0, piece), :], u_all) - _dot(s_ref[pl.ds(r0, piece), :], v_all)
        tiles = slice(k * piece // SUBLANES, (k + 1) * piece // SUBLANES)
        for s in range(nb):
            y_ref[tiles, s] = y[:, s * width:(s + 1) * width].reshape(piece // SUBLANES, SUBLANES, width)

    def from_prev_chunk(v):
        n = v.shape[0] * flat
        r = pltpu.roll(v.reshape(n, hd), 1, 0).reshape(v.shape)
        return jnp.where(sub[None] == 0, 0.0, r)

    def from_next_chunk(v):
        n = v.shape[0] * flat
        r = pltpu.roll(v.reshape(n, hd), n - 1, 0).reshape(v.shape)
        return jnp.where(sub[None] == SUBLANES - 1, 0.0, r)

    def shifted(t0, off, n):
        lo, hi = t0 + off, t0 + off + n
        parts = []
        if lo < 0:
            parts.append(from_prev_chunk(x_ref[lc + lo:lc + min(hi, 0)]))
        if hi > 0 and lo < lc:
            parts.append(x_ref[max(lo, 0):min(hi, lc)])
        if hi > lc:
            parts.append(from_next_chunk(x_ref[max(lo, lc) - lc:hi - lc]))
        return parts[0] if len(parts) == 1 else jnp.concatenate(parts, axis=0)

    z = -lam_ref[...]
    softplus = jnp.maximum(z, 0.0) + jnp.log1p(jnp.exp(-jnp.abs(z)))
    half_l2 = (-0.5 * LRU_C * LOG2_E) * softplus
    half_ba = 0.5 * ba_ref[...]
    half_bx = 0.5 * bx_ref[...]

    for t0 in range(0, lc, ct):
        xc = cb_ref[...][None, None]
        for k in range(CONV_W):
            xc = xc + shifted(t0, k - CONV_LEFT, ct) * cw_ref[k:k + 1, :][None, None]
        xc2 = xc.reshape(ct * flat, hd)
        half_xc = 0.5 * xc2
        g = _dot(xc2.astype(BF16), wg_ref[...])
        fourier_piece(t0 // ct)
        for d in range(2):
            t_r = jnp.tanh(g[:, 2 * d * hd:(2 * d + 1) * hd] + half_ba[d])
            t_i = jnp.tanh(g[:, (2 * d + 1) * hd:(2 * d + 2) * hd] + half_bx[d])
            a = jnp.exp2(half_l2[d] * t_r + half_l2[d])
            m = 1.0 - a * a
            mult = jnp.where(m > 0.0, m * lax.rsqrt(m), 0.0)
            a_s[d, t0:t0 + ct] = a.reshape(ct, *tile)
            b_s[d, t0:t0 + ct] = ((mult * half_xc) * (t_i + 1.0)).reshape(ct, *tile)

    def scan_body(s, carry):
        hf, pf, hb, pb = carry
        tb = lc - 1 - s
        af, bf = a_s[0, s], b_s[0, s]
        ab, bb = a_s[1, tb], b_s[1, tb]
        hf = af * hf + bf
        pf = af * pf
        hb = ab * hb + bb
        pb = ab * pb
        h_s[0, s] = hf
        p_s[0, s] = pf
        h_s[1, tb] = hb
        p_s[1, tb] = pb
        return hf, pf, hb, pb

    zero, one = jnp.zeros(tile, F32), jnp.ones(tile, F32)
    hf, pf, hb, pb = lax.fori_loop(0, lc, scan_body, (zero, one, zero, one), unroll=8)

    def chain(h_end, p_end, h0, forward):
        shift, edge = (1, 0) if forward else (flat - 1, SUBLANES - 1)
        state = h0
        for _ in range(SUBLANES - 1):
            nxt = pltpu.roll((p_end * state + h_end).reshape(flat, hd), shift, 0).reshape(tile)
            state = jnp.where(sub == edge, h0, nxt)
        return state

    sf = chain(hf, pf, h0_ref[0], True)
    sb = chain(hb, pb, h0_ref[1], False)
    st_ref[0] = pf * sf + hf
    st_ref[1] = pb * sb + hb

    for t0 in range(0, lc, ct):
        sl = slice(t0, t0 + ct)
        out_ref[sl] = (h_s[0, sl] + p_s[0, sl] * sf[None]) + (h_s[1, sl] + p_s[1, sl] * sb[None])


def _seq_mix_call(l, lead_block, lc, nb, xr4, uv4, conv_w, conv_b, wg, ba, bx, lam, h0, cmat, smat, ct, name,
                  casts=()):
    n_tiles, n_seq, _, d_rnn = xr4.shape
    hd = d_rnn // N_LRU_HEADS
    assert N_LRU_HEADS == 2 * (uv4.shape[-1] // (2 * MXU_DIM))
    seq_len = lc * SUBLANES
    blk = (lc, nb, SUBLANES, hd)
    n_steps = n_seq // nb * N_LRU_HEADS
    step = lambda b, h: b * N_LRU_HEADS + h
    cast_in_specs, cast_out_specs, cast_out_shapes = [], [], []
    for w, src_layer in casts:
        k, n = w.shape[1:]
        assert k % (n_steps * 2 * SUBLANES) == 0
        cast_in_specs.append(pl.BlockSpec((None, k // n_steps, n), lambda b, h, j=src_layer: (j, step(b, h), 0)))
        cast_out_specs.append(pl.BlockSpec((k // n_steps, n), lambda b, h: (step(b, h), 0)))
        cast_out_shapes.append(jax.ShapeDtypeStruct((k, n), BF16))
    in_specs = [
        pl.BlockSpec(blk, lambda b, h: (lead_block, b, 0, h)),
        pl.BlockSpec((None, CONV_W, hd), lambda b, h: (l, 0, h)),
        pl.BlockSpec((None, 1, hd), lambda b, h: (l, 0, h)),
        pl.BlockSpec((None, None, hd, 4 * hd), lambda b, h: (l, h, 0, 0)),
        pl.BlockSpec((None, 2, 1, hd), lambda b, h: (l, 0, 0, h)),
        pl.BlockSpec((None, 2, 1, hd), lambda b, h: (l, 0, 0, h)),
        pl.BlockSpec((None, 2, 1, hd), lambda b, h: (l, 0, 0, h)),
        pl.BlockSpec((None, 2, nb, SUBLANES, hd), lambda b, h: (l, 0, b, 0, h)),
        pl.BlockSpec((lc, nb, SUBLANES, 2 * MXU_DIM), lambda b, h: (lead_block, b, 0, h // 2)),
        _resident((seq_len, seq_len), lambda b, h: (0, 0)),
        _resident((seq_len, seq_len), lambda b, h: (0, 0)),
    ] + cast_in_specs
    return pl.pallas_call(
        functools.partial(_seq_mix_kernel, ct, len(casts)),
        grid=(n_seq // nb, N_LRU_HEADS),
        in_specs=in_specs,
        out_specs=[
            pl.BlockSpec(blk, lambda b, h: (lead_block, b, 0, h)),
            pl.BlockSpec((2, nb, SUBLANES, hd), lambda b, h: (0, b, 0, h)),
            pl.BlockSpec((lc // 2, nb, SUBLANES, MXU_DIM), lambda b, h: (h % 2, b, 0, h // 2)),
        ] + cast_out_specs,
        out_shape=[
            jax.ShapeDtypeStruct(xr4.shape, F32),
            jax.ShapeDtypeStruct((2, n_seq, SUBLANES, d_rnn), F32),
            jax.ShapeDtypeStruct((lc, n_seq, SUBLANES, uv4.shape[-1] // 2), F32),
        ] + cast_out_shapes,
        scratch_shapes=[
            pltpu.VMEM((2, lc, nb, SUBLANES, hd), F32),
            pltpu.VMEM((2, lc, nb, SUBLANES, hd), F32),
            pltpu.VMEM((2, lc, nb, SUBLANES, hd), F32),
            pltpu.VMEM((2, lc, nb, SUBLANES, hd), F32),
        ],
        input_output_aliases={0: 0},
        compiler_params=_params(("arbitrary", "arbitrary")),
        name=name,
    )(xr4, conv_w, conv_b, wg, ba, bx, lam, h0, uv4, cmat, smat, *[w for w, _ in casts])


def _mix_ffn_kernel(d_ff, lat_tiles, final, split_x, x_ref, *rest):
    xctx_ref = rest[0] if split_x else None
    (rec_ref, gy_ref, ylat_ref, yctx_ref, gate_ref, g1_ref, sh2_ref, sc2_ref, g2_ref, n2_ref,
     wl_ref, wf_ref, wo_ref, w1_ref, w2_ref) = rest[1 if split_x else 0:][:15]
    rest = rest[(1 if split_x else 0) + 15:]
    fin_ref = rest[0] if final else None
    outs, act_s = rest[1 if final else 0:-1], rest[-1]
    d = x_ref.shape[1]
    is_lat = pl.program_id(0) < lat_tiles
    out_a = _dot((rec_ref[...] * gy_ref[...]).astype(BF16), wl_ref[...])
    out_b = _dot(_group_tile(lat_tiles, ylat_ref, yctx_ref).astype(BF16), wf_ref[...])
    merged = gate_ref[:, 0:d] * out_a + gate_ref[:, d:2 * d] * out_b
    x1 = _gated_add(_group_tile(lat_tiles, x_ref, xctx_ref), g1_ref, _dot(merged.astype(BF16), wo_ref[...]))
    h2 = _modulate(_rmsnorm(x1, n2_ref), sc2_ref, sh2_ref).astype(BF16)
    for c0, c1 in _ffn_chunks(d_ff):
        u = _dot(h2, w1_ref[:, c0:c1])
        v = _dot(h2, w1_ref[:, d_ff + c0:d_ff + c1])
        act_s[:, c0:c1] = ((u * _sigmoid(u)) * v).astype(BF16)
    x2 = _gated_add(x1, g2_ref, _dot(act_s[...], w2_ref[...]))
    if fin_ref is None:
        outs[0][...] = x2
    else:
        xn = _rmsnorm(x2, fin_ref)

        @pl.when(is_lat)
        def _():
            outs[0][...] = xn

        @pl.when(jnp.logical_not(is_lat))
        def _():
            outs[1][...] = xn


def _ffn_chunks(d_ff):
    n_tiles = d_ff // MXU_DIM
    half = (n_tiles + 1) // 2 * MXU_DIM
    return [(0, half), (half, d_ff)]


def _mix_ffn_call(l, x, rec, gy, y_lat, y_ctx, gate, mod, norm_g, wl, wf, wo, w1, w2, tm, lat_tiles, final_g):
    xs = x if isinstance(x, tuple) else (x,)
    mod, mod_l = mod
    rows, d = sum(a.shape[0] for a in xs), xs[0].shape[1]
    d_rnn = rec.shape[1]
    d_fnet, d_ff = wf.shape[0], w2.shape[0]
    n_tiles = rows // tm
    grp = lambda i: jnp.where(i < lat_tiles, 0, 1)
    tile = lambda w: pl.BlockSpec((tm, w), lambda i: (i, 0))
    modspec = lambda j: pl.BlockSpec((None, MOD_ROWS, d), lambda i: (mod_l, grp(i), j))
    in_specs = (_group_specs(tm, d, lat_tiles) if len(xs) == 2 else [tile(d)]) + [
        tile(d_rnn), tile(d_rnn), *_group_specs(tm, d_fnet, lat_tiles), tile(2 * d),
        modspec(2), modspec(3), modspec(4), modspec(5),
        pl.BlockSpec((None, 1, d), lambda i: (l, 0, 0)),
    ] + [_resident(w.shape, lambda i: (0, 0)) for w in (wl, wf, wo, w1, w2)]
    args = [*xs, rec, gy, y_lat, y_ctx, gate, mod, mod, mod, mod, norm_g, wl, wf, wo, w1, w2]
    if final_g is None:
        out_specs = tile(d)
        out_shape = jax.ShapeDtypeStruct((rows, d), F32)
    else:
        in_specs.append(pl.BlockSpec((1, d), lambda i: (0, 0)))
        args.append(final_g)
        out_specs = _group_specs(tm, d, lat_tiles)
        out_shape = [
            jax.ShapeDtypeStruct((lat_tiles * tm, d), F32),
            jax.ShapeDtypeStruct(((n_tiles - lat_tiles) * tm, d), F32),
        ]
    return pl.pallas_call(
        functools.partial(_mix_ffn_kernel, d_ff, lat_tiles, final_g is not None, len(xs) == 2),
        grid=(n_tiles,),
        in_specs=in_specs,
        out_specs=out_specs,
        out_shape=out_shape,
        scratch_shapes=[pltpu.VMEM((tm, d_ff), BF16)],
        compiler_params=_params(("arbitrary",)),
        name="mix_ffn",
    )(*args)


def _cos_sin(n):
    k = np.arange(n)
    ang = 2.0 * np.pi * ((k[:, None] * k[None, :]) % n) / n
    return np.cos(ang), np.sin(ang)


def _channel_dft(d_fnet):
    gd = d_fnet // N_FNET_GROUPS
    c, s = _cos_sin(gd)
    eye = np.eye(N_FNET_GROUPS)
    scale = 1.0 / np.sqrt(gd)
    u, v = np.kron(eye, c) * scale, np.kron(eye, s) * scale
    blocks = []
    for j in range(0, d_fnet, MXU_DIM):
        blocks += [u[:, j:j + MXU_DIM], v[:, j:j + MXU_DIM]]
    return np.concatenate(blocks, axis=1)


def _row_order(seq_len):
    j = np.arange(seq_len)
    return (j % SUBLANES) * (seq_len // SUBLANES) + j // SUBLANES


def _seq_dft(n):
    c, s = _cos_sin(n)
    scale = 1.0 / np.sqrt(n)
    p = _row_order(n)
    return (c * scale)[p][:, p], (s * scale)[p][:, p]


def _grid_dft(n_rows, n_cols):
    cr, sr = _cos_sin(n_rows)
    cw, sw = _cos_sin(n_cols)
    scale = 1.0 / np.sqrt(n_rows * n_cols)
    p = _row_order(n_rows * n_cols)
    c = (np.kron(cr, cw) - np.kron(sr, sw)) * scale
    s = (np.kron(sr, cw) + np.kron(cr, sw)) * scale
    return c[p][:, p], s[p][:, p]


def _to_chunk_layout(x):
    b, l, d = x.shape
    return x.reshape(b, SUBLANES, l // SUBLANES, d).transpose(2, 0, 1, 3).reshape(b * l, d)


def _from_chunk_layout(rows, b, l):
    d = rows.shape[-1]
    return rows.reshape(l // SUBLANES, b, SUBLANES, d).transpose(1, 2, 0, 3).reshape(b, l, d)


def kernel(x_prompt, x_sample, state_lru, c, c_ctx, norm1_g, norm2_g, ada_w, ada_b, w_in, b_in, conv_w, conv_b,
           lru_wa, lru_ba, lru_wx, lru_bx, lru_lambda, w_lru_out, w_fnet_out, w_out, ffn_w_in, ffn_w_out,
           final_g):
    n_ctx, l_ctx, d = x_prompt.shape
    n_lat, l_lat, _ = x_sample.shape
    depth = w_in.shape[0]
    d_rnn = conv_w.shape[-1]
    d_fnet = w_fnet_out.shape[1]
    n_in = w_in.shape[-1]
    assert MOD_ROWS % (n_lat * SUBLANES) == 0 and MOD_ROWS % (n_ctx * SUBLANES) == 0
    assert l_lat % GRID_W == 0 and n_in == 2 * d_rnn + d_fnet + 2 * d
    rows_lat, rows_ctx = n_lat * l_lat, n_ctx * l_ctx
    rows = rows_lat + rows_ctx
    assert rows_lat % rows_ctx == 0
    lc_lat, lc_ctx = l_lat // SUBLANES, l_ctx // SUBLANES
    ctx_block = rows_lat // rows_ctx

    tm_in, tm_ffn = 1024, 512
    tm_in_first = 512

    x = (_to_chunk_layout(x_sample), _to_chunk_layout(x_prompt))

    c_pat = jnp.concatenate([
        jnp.tile(jnp.repeat(c, SUBLANES, axis=0), (MOD_ROWS // (n_lat * SUBLANES), 1)),
        jnp.broadcast_to(c_ctx[None, :], (MOD_ROWS, d))], axis=0)
    mod_first = _ada_call(c_pat, ada_w, ada_b, 0, 1)

    w_in_b = w_in[0].astype(BF16)
    wg = (0.5 * jnp.concatenate([lru_wa[:, 0], lru_wx[:, 0], lru_wa[:, 1], lru_wx[:, 1]], axis=-1)).astype(BF16)
    b_in3 = b_in.reshape(depth, 1, n_in)
    n1 = norm1_g.reshape(depth, 1, d)
    n2 = norm2_g.reshape(depth, 1, d)
    conv_b3 = conv_b.reshape(depth, 1, d_rnn)
    ba4 = lru_ba.reshape(depth, 2, 1, d_rnn)
    bx4 = lru_bx.reshape(depth, 2, 1, d_rnn)
    lam4 = lru_lambda.reshape(depth, 2, 1, d_rnn)
    h0_lat = jnp.broadcast_to(state_lru.transpose(1, 2, 0, 3)[:, :, :, None, :],
                              (depth, 2, n_lat, SUBLANES, d_rnn))
    h0_ctx = jnp.zeros((depth, 2, n_ctx, SUBLANES, d_rnn), F32)

    wc = jnp.asarray(_channel_dft(d_fnet), F32).astype(BF16)
    c_ctx_m, s_ctx_m = (jnp.asarray(m, F32).astype(BF16) for m in _seq_dft(l_ctx))
    c_lat_m, s_lat_m = (jnp.asarray(m, F32).astype(BF16) for m in _grid_dft(l_lat // GRID_W, GRID_W))

    as_lat = lambda a: a.reshape(-1, n_lat, SUBLANES, a.shape[-1])
    as_ctx = lambda a: a.reshape(-1, n_ctx, SUBLANES, a.shape[-1])

    states = []
    for l in range(depth):
        tm = tm_in_first if l == 0 else tm_in
        mod = (mod_first, 0) if l == 0 else (mod_rest, l - 1)
        xr, gy, uv, gate = _inproj_call(l, x, mod, n1, w_in_b, b_in3, wc, tm, rows_lat // tm, d_rnn, d_fnet)
        if l == 0 and depth > 1:
            mod_rest = _ada_call(c_pat + 0.0 * gate[:1, :1].astype(F32), ada_w, ada_b, 1, depth - 1)

        rec, _, y_lat, w1_b, wl_b, wo_b = _seq_mix_call(l, 0, lc_lat, 1, as_lat(xr), as_lat(uv), conv_w, conv_b3, wg,
                                                        ba4, bx4, lam4, h0_lat, c_lat_m, s_lat_m, 64, "seq_mix_lat",
                                                        [(ffn_w_in, l), (w_lru_out, l), (w_out, l)])
        casts = [(w_fnet_out, l), (ffn_w_out, l)]
        if l + 1 < depth:
            casts.append((w_in, l + 1))
        rec, st, y_ctx, *w_b = _seq_mix_call(l, ctx_block, lc_ctx, 4, as_ctx(rec), as_ctx(uv), conv_w, conv_b3, wg,
                                             ba4, bx4, lam4, h0_ctx, c_ctx_m, s_ctx_m, 16, "seq_mix_ctx", casts)
        states.append(jnp.stack([st[0, :, SUBLANES - 1], st[1, :, 0]], axis=1))

        x = _mix_ffn_call(l, x, rec.reshape(rows, d_rnn), gy, y_lat.reshape(rows_lat, d_fnet),
                          y_ctx.reshape(rows_ctx, d_fnet), gate, mod, n2,
                          wl_b, w_b[0], wo_b, w1_b, w_b[1], tm_ffn, rows_lat // tm_ffn,
                          final_g.reshape(1, d) if l == depth - 1 else None)
        if l + 1 < depth:
            w_in_b = w_b[2]

    y_sample = _from_chunk_layout(x[0], n_lat, l_lat)
    y_prompt = _from_chunk_layout(x[1], n_ctx, l_ctx)
    new_state = jnp.stack(states, axis=1)
    return (y_prompt, y_sample, new_state)
```

```python
import functools

import numpy as np
import jax
import jax.numpy as jnp
from jax import lax
from jax.experimental import pallas as pl
from jax.experimental.pallas import tpu as pltpu

F32 = jnp.float32
BF16 = jnp.bfloat16

GRID_W = 64
N_LRU_HEADS = 4
LRU_C = 8.0
CONV_W = 4
CONV_LEFT = 2
N_FNET_GROUPS = 8
EPS = 1e-6
LOG2_E = 1.4426950408889634

SUBLANES = 8
MXU_DIM = 256
MOD_ROWS = 128
VMEM_LIMIT = 60 * 1024 * 1024


def _sigmoid(x):
    return 0.5 * jnp.tanh(0.5 * x) + 0.5


def _dot(a, b):
    return jnp.dot(a, b, preferred_element_type=F32)


def _params(sem):
    return pltpu.CompilerParams(dimension_semantics=sem, vmem_limit_bytes=VMEM_LIMIT)


def _resident(block_shape, index_map):
    return pl.BlockSpec(block_shape, index_map, pipeline_mode=pl.Buffered(1))


def _modulate(y, scale_ref, shift_ref):
    rows, d = y.shape
    y3 = y.reshape(rows // MOD_ROWS, MOD_ROWS, d)
    y3 = y3 * (1.0 + scale_ref[...])[None] + shift_ref[...][None]
    return y3.reshape(rows, d)


def _gated_add(x, gate_ref, upd):
    rows, d = x.shape
    x3 = x.reshape(rows // MOD_ROWS, MOD_ROWS, d)
    u3 = upd.reshape(rows // MOD_ROWS, MOD_ROWS, d)
    return (x3 + gate_ref[...][None] * u3).reshape(rows, d)


def _rmsnorm(x, g_ref):
    ms = jnp.mean(x * x, axis=-1, keepdims=True)
    return x * lax.rsqrt(ms + EPS) * g_ref[...]


def _ada_kernel(c_ref, w_ref, b_ref, o_ref):
    @pl.when(pl.program_id(1) == 0)
    def _():
        o_ref[...] = jnp.broadcast_to(b_ref[...], o_ref.shape)

    c = c_ref[...]
    cs = c * _sigmoid(c)
    o_ref[...] += _dot(cs.astype(BF16), w_ref[...].astype(BF16))


def _ada_call(c_pat, ada_w, ada_b):
    depth, d, n6 = ada_w.shape
    rows = c_pat.shape[0]
    tk = MXU_DIM
    return pl.pallas_call(
        _ada_kernel,
        grid=(depth, d // tk),
        in_specs=[
            pl.BlockSpec((rows, tk), lambda l, k: (0, k)),
            pl.BlockSpec((None, tk, n6), lambda l, k: (l, k, 0)),
            pl.BlockSpec((None, 1, n6), lambda l, k: (l, 0, 0)),
        ],
        out_specs=pl.BlockSpec((None, rows, n6), lambda l, k: (l, 0, 0)),
        out_shape=jax.ShapeDtypeStruct((depth, rows, n6), F32),
        compiler_params=_params(("arbitrary", "arbitrary")),
        name="ada_mod",
    )(c_pat, ada_w, ada_b.reshape(depth, 1, n6))


def _group_tile(lat_tiles, lat_ref, ctx_ref):
    if ctx_ref is None:
        return lat_ref[...]
    return jnp.where(pl.program_id(0) < lat_tiles, lat_ref[...], ctx_ref[...])


def _group_specs(tm, width, lat_tiles):
    return [pl.BlockSpec((tm, width), lambda i: (jnp.minimum(i, lat_tiles - 1), 0)),
            pl.BlockSpec((tm, width), lambda i: (jnp.maximum(i - lat_tiles, 0), 0))]


def _inproj_kernel(d_rnn, d_fnet, lat_tiles, split_x, x_ref, *rest):
    xctx_ref = rest[0] if split_x else None
    sh_ref, sc_ref, g_ref, w_ref, b_ref, wc_ref, xr_ref, gy_ref, uv_ref, gate_ref = rest[1 if split_x else 0:]
    h = _modulate(_rmsnorm(_group_tile(lat_tiles, x_ref, xctx_ref), g_ref), sc_ref, sh_ref).astype(BF16)
    o1, o2, o3 = d_rnn, 2 * d_rnn, 2 * d_rnn + d_fnet
    n_in = w_ref.shape[1]
    gate_ref[...] = _sigmoid(_dot(h, w_ref[:, o3:n_in]) + b_ref[:, o3:n_in]).astype(BF16)
    gy_ref[...] = jax.nn.gelu(_dot(h, w_ref[:, o1:o2]) + b_ref[:, o1:o2]).astype(BF16)
    xf = _dot(h, w_ref[:, o2:o3]) + b_ref[:, o2:o3]
    uv_ref[...] = _dot(xf.astype(BF16), wc_ref[...])
    xr_ref[...] = _dot(h, w_ref[:, 0:o1]) + b_ref[:, 0:o1]


def _inproj_call(l, x, mod, norm_g, w_in, b_in, wc, tm, lat_tiles, d_rnn, d_fnet):
    xs = x if isinstance(x, tuple) else (x,)
    rows, d = sum(a.shape[0] for a in xs), xs[0].shape[1]
    n_in = w_in.shape[1]
    n_gate = n_in - 2 * d_rnn - d_fnet
    grp = lambda i: jnp.where(i < lat_tiles, 0, 1)
    x_specs = _group_specs(tm, d, lat_tiles) if len(xs) == 2 else [pl.BlockSpec((tm, d), lambda i: (i, 0))]
    return pl.pallas_call(
        functools.partial(_inproj_kernel, d_rnn, d_fnet, lat_tiles, len(xs) == 2),
        grid=(rows // tm,),
        in_specs=x_specs + [
            pl.BlockSpec((None, MOD_ROWS, d), lambda i: (l, grp(i), 0)),
            pl.BlockSpec((None, MOD_ROWS, d), lambda i: (l, grp(i), 1)),
            pl.BlockSpec((None, 1, d), lambda i: (l, 0, 0)),
            _resident((d, n_in), lambda i: (0, 0)),
            pl.BlockSpec((None, 1, n_in), lambda i: (l, 0, 0)),
            _resident((d_fnet, 2 * d_fnet), lambda i: (0, 0)),
        ],
        out_specs=[
            pl.BlockSpec((tm, d_rnn), lambda i: (i, 0)),
            pl.BlockSpec((tm, d_rnn), lambda i: (i, 0)),
            pl.BlockSpec((tm, 2 * d_fnet), lambda i: (i, 0)),
            pl.BlockSpec((tm, n_gate), lambda i: (i, 0)),
        ],
        out_shape=[
            jax.ShapeDtypeStruct((rows, d_rnn), F32),
            jax.ShapeDtypeStruct((rows, d_rnn), BF16),
            jax.ShapeDtypeStruct((rows, 2 * d_fnet), F32),
            jax.ShapeDtypeStruct((rows, n_gate), BF16),
        ],
        compiler_params=_params(("arbitrary",)),
        name="in_proj",
    )(*xs, mod, mod, norm_g, w_in, b_in, wc)


def _seq_mix_kernel(ct, n_cast, x_ref, cw_ref, cb_ref, wg_ref, ba_ref, bx_ref, lam_ref, h0_ref,
                    uv_ref, c_ref, s_ref, *rest):
    cast_in, (out_ref, st_ref, y_ref), cast_out = rest[:n_cast], rest[n_cast:n_cast + 3], rest[n_cast + 3:-4]
    a_s, b_s, h_s, p_s = rest[-4:]
    for w_ref, wb_ref in zip(cast_in, cast_out):
        wb_ref[...] = w_ref[...].astype(BF16)
    lc, nb, _, hd = x_ref.shape
    tile = (nb, SUBLANES, hd)
    flat = nb * SUBLANES
    sub = lax.broadcasted_iota(jnp.int32, tile, 1)

    width = y_ref.shape[-1]
    seq_rows = lc * SUBLANES
    n_chunks = lc // ct
    piece = seq_rows // 2 // n_chunks
    side_by_side = lambda lo: jnp.concatenate(
        [uv_ref[:, s, :, lo:lo + width].reshape(seq_rows, width) for s in range(nb)], axis=1).astype(BF16)
    u_all, v_all = side_by_side(0), side_by_side(width)

    def fourier_piece(k):
        r0 = pl.multiple_of((pl.program_id(1) % 2) * (seq_rows // 2) + k * piece, piece)
        y = _dot(c_ref[pl.ds(r0, piece), :], u_all) - _dot(s_ref[pl.ds(r0, piece), :], v_all)
        tiles = slice(k * piece // SUBLANES, (k + 1) * piece // SUBLANES)
        for s in range(nb):
            y_ref[tiles, s] = y[:, s * width:(s + 1) * width].reshape(piece // SUBLANES, SUBLANES, width)

    def from_prev_chunk(v):
        n = v.shape[0] * flat
        r = pltpu.roll(v.reshape(n, hd), 1, 0).reshape(v.shape)
        return jnp.where(sub[None] == 0, 0.0, r)

    def from_next_chunk(v):
        n = v.shape[0] * flat
        r = pltpu.roll(v.reshape(n, hd), n - 1, 0).reshape(v.shape)
        return jnp.where(sub[None] == SUBLANES - 1, 0.0, r)

    def shifted(t0, off, n):
        lo, hi = t0 + off, t0 + off + n
        parts = []
        if lo < 0:
            parts.append(from_prev_chunk(x_ref[lc + lo:lc + min(hi, 0)]))
        if hi > 0 and lo < lc:
            parts.append(x_ref[max(lo, 0):min(hi, lc)])
        if hi > lc:
            parts.append(from_next_chunk(x_ref[max(lo, lc) - lc:hi - lc]))
        return parts[0] if len(parts) == 1 else jnp.concatenate(parts, axis=0)

    z = -lam_ref[...]
    softplus = jnp.maximum(z, 0.0) + jnp.log1p(jnp.exp(-jnp.abs(z)))
    half_l2 = (-0.5 * LRU_C * LOG2_E) * softplus
    half_ba = 0.5 * ba_ref[...]
    half_bx = 0.5 * bx_ref[...]

    for t0 in range(0, lc, ct):
        xc = cb_ref[...][None, None]
        for k in range(CONV_W):
            xc = xc + shifted(t0, k - CONV_LEFT, ct) * cw_ref[k:k + 1, :][None, None]
        xc2 = xc.reshape(ct * flat, hd)
        half_xc = 0.5 * xc2
        g = _dot(xc2.astype(BF16), wg_ref[...])
        fourier_piece(t0 // ct)
        for d in range(2):
            t_r = jnp.tanh(g[:, 2 * d * hd:(2 * d + 1) * hd] + half_ba[d])
            t_i = jnp.tanh(g[:, (2 * d + 1) * hd:(2 * d + 2) * hd] + half_bx[d])
            a = jnp.exp2(half_l2[d] * t_r + half_l2[d])
            m = 1.0 - a * a
            mult = jnp.where(m > 0.0, m * lax.rsqrt(m), 0.0)
            a_s[d, t0:t0 + ct] = a.reshape(ct, *tile)
            b_s[d, t0:t0 + ct] = ((mult * half_xc) * (t_i + 1.0)).reshape(ct, *tile)

    def scan_body(s, carry):
        hf, pf, hb, pb = carry
        tb = lc - 1 - s
        af, bf = a_s[0, s], b_s[0, s]
        ab, bb = a_s[1, tb], b_s[1, tb]
        hf = af * hf + bf
        pf = af * pf
        hb = ab * hb + bb
        pb = ab * pb
        h_s[0, s] = hf
        p_s[0, s] = pf
        h_s[1, tb] = hb
        p_s[1, tb] = pb
        return hf, pf, hb, pb

    zero, one = jnp.zeros(tile, F32), jnp.ones(tile, F32)
    hf, pf, hb, pb = lax.fori_loop(0, lc, scan_body, (zero, one, zero, one), unroll=8)

    def chain(h_end, p_end, h0, forward):
        shift, edge = (1, 0) if forward else (flat - 1, SUBLANES - 1)
        state = h0
        for _ in range(SUBLANES - 1):
            nxt = pltpu.roll((p_end * state + h_end).reshape(flat, hd), shift, 0).reshape(tile)
            state = jnp.where(sub == edge, h0, nxt)
        return state

    sf = chain(hf, pf, h0_ref[0], True)
    sb = chain(hb, pb, h0_ref[1], False)
    st_ref[0] = pf * sf + hf
    st_ref[1] = pb * sb + hb

    for t0 in range(0, lc, ct):
        sl = slice(t0, t0 + ct)
        out_ref[sl] = (h_s[0, sl] + p_s[0, sl] * sf[None]) + (h_s[1, sl] + p_s[1, sl] * sb[None])


def _seq_mix_call(l, lead_block, lc, nb, xr4, uv4, conv_w, conv_b, wg, ba, bx, lam, h0, cmat, smat, ct, name,
                  casts=()):
    n_tiles, n_seq, _, d_rnn = xr4.shape
    hd = d_rnn // N_LRU_HEADS
    assert N_LRU_HEADS == 2 * (uv4.shape[-1] // (2 * MXU_DIM))
    seq_len = lc * SUBLANES
    blk = (lc, nb, SUBLANES, hd)
    n_steps = n_seq // nb * N_LRU_HEADS
    step = lambda b, h: b * N_LRU_HEADS + h
    cast_in_specs, cast_out_specs, cast_out_shapes = [], [], []
    for w, src_layer in casts:
        k, n = w.shape[1:]
        assert k % (n_steps * 2 * SUBLANES) == 0
        cast_in_specs.append(pl.BlockSpec((None, k // n_steps, n), lambda b, h, j=src_layer: (j, step(b, h), 0)))
        cast_out_specs.append(pl.BlockSpec((k // n_steps, n), lambda b, h: (step(b, h), 0)))
        cast_out_shapes.append(jax.ShapeDtypeStruct((k, n), BF16))
    in_specs = [
        pl.BlockSpec(blk, lambda b, h: (lead_block, b, 0, h)),
        pl.BlockSpec((None, CONV_W, hd), lambda b, h: (l, 0, h)),
        pl.BlockSpec((None, 1, hd), lambda b, h: (l, 0, h)),
        pl.BlockSpec((None, None, hd, 4 * hd), lambda b, h: (l, h, 0, 0)),
        pl.BlockSpec((None, 2, 1, hd), lambda b, h: (l, 0, 0, h)),
        pl.BlockSpec((None, 2, 1, hd), lambda b, h: (l, 0, 0, h)),
        pl.BlockSpec((None, 2, 1, hd), lambda b, h: (l, 0, 0, h)),
        pl.BlockSpec((None, 2, nb, SUBLANES, hd), lambda b, h: (l, 0, b, 0, h)),
        pl.BlockSpec((lc, nb, SUBLANES, 2 * MXU_DIM), lambda b, h: (lead_block, b, 0, h // 2)),
        _resident((seq_len, seq_len), lambda b, h: (0, 0)),
        _resident((seq_len, seq_len), lambda b, h: (0, 0)),
    ] + cast_in_specs
    return pl.pallas_call(
        functools.partial(_seq_mix_kernel, ct, len(casts)),
        grid=(n_seq // nb, N_LRU_HEADS),
        in_specs=in_specs,
        out_specs=[
            pl.BlockSpec(blk, lambda b, h: (lead_block, b, 0, h)),
            pl.BlockSpec((2, nb, SUBLANES, hd), lambda b, h: (0, b, 0, h)),
            pl.BlockSpec((lc // 2, nb, SUBLANES, MXU_DIM), lambda b, h: (h % 2, b, 0, h // 2)),
        ] + cast_out_specs,
        out_shape=[
            jax.ShapeDtypeStruct(xr4.shape, F32),
            jax.ShapeDtypeStruct((2, n_seq, SUBLANES, d_rnn), F32),
            jax.ShapeDtypeStruct((lc, n_seq, SUBLANES, uv4.shape[-1] // 2), F32),
        ] + cast_out_shapes,
        scratch_shapes=[
            pltpu.VMEM((2, lc, nb, SUBLANES, hd), F32),
            pltpu.VMEM((2, lc, nb, SUBLANES, hd), F32),
            pltpu.VMEM((2, lc, nb, SUBLANES, hd), F32),
            pltpu.VMEM((2, lc, nb, SUBLANES, hd), F32),
        ],
        input_output_aliases={0: 0},
        compiler_params=_params(("arbitrary", "arbitrary")),
        name=name,
    )(xr4, conv_w, conv_b, wg, ba, bx, lam, h0, uv4, cmat, smat, *[w for w, _ in casts])


def _mix_ffn_kernel(d_ff, lat_tiles, final, split_x, x_ref, *rest):
    xctx_ref = rest[0] if split_x else None
    (rec_ref, gy_ref, ylat_ref, yctx_ref, gate_ref, g1_ref, sh2_ref, sc2_ref, g2_ref, n2_ref,
     wl_ref, wf_ref, wo_ref, w1_ref, w2_ref) = rest[1 if split_x else 0:][:15]
    rest = rest[(1 if split_x else 0) + 15:]
    fin_ref = rest[0] if final else None
    outs, act_s = rest[1 if final else 0:-1], rest[-1]
    d = x_ref.shape[1]
    is_lat = pl.program_id(0) < lat_tiles
    tm = x_ref.shape[0]
    x_all = _group_tile(lat_tiles, x_ref, xctx_ref)
    y_all = _group_tile(lat_tiles, ylat_ref, yctx_ref)
    x1_parts, h2_parts = [], []
    for r0 in range(0, tm, tm // 2):
        rs = slice(r0, r0 + tm // 2)
        out_a = _dot((rec_ref[rs, :] * gy_ref[rs, :]).astype(BF16), wl_ref[...])
        out_b = _dot(y_all[rs, :].astype(BF16), wf_ref[...])
        merged = gate_ref[rs, 0:d] * out_a + gate_ref[rs, d:2 * d] * out_b
        x1_half = _gated_add(x_all[rs, :], g1_ref, _dot(merged.astype(BF16), wo_ref[...]))
        x1_parts.append(x1_half)
        h2_parts.append(_modulate(_rmsnorm(x1_half, n2_ref), sc2_ref, sh2_ref).astype(BF16))
    x1 = jnp.concatenate(x1_parts, axis=0)
    h2 = jnp.concatenate(h2_parts, axis=0)
    for c0, c1 in _ffn_chunks(d_ff):
        u = _dot(h2, w1_ref[:, c0:c1])
        v = _dot(h2, w1_ref[:, d_ff + c0:d_ff + c1])
        act_s[:, c0:c1] = ((u * _sigmoid(u)) * v).astype(BF16)
    x2 = _gated_add(x1, g2_ref, _dot(act_s[...], w2_ref[...]))
    if fin_ref is None:
        outs[0][...] = x2
    else:
        xn = _rmsnorm(x2, fin_ref)

        @pl.when(is_lat)
        def _():
            outs[0][...] = xn

        @pl.when(jnp.logical_not(is_lat))
        def _():
            outs[1][...] = xn


def _ffn_chunks(d_ff):
    n_tiles = d_ff // MXU_DIM
    half = (n_tiles + 1) // 2 * MXU_DIM
    return [(0, half), (half, d_ff)]


def _mix_ffn_call(l, x, rec, gy, y_lat, y_ctx, gate, mod, norm_g, wl, wf, wo, w1, w2, tm, lat_tiles, final_g):
    xs = x if isinstance(x, tuple) else (x,)
    rows, d = sum(a.shape[0] for a in xs), xs[0].shape[1]
    d_rnn = rec.shape[1]
    d_fnet, d_ff = wf.shape[0], w2.shape[0]
    n_tiles = rows // tm
    grp = lambda i: jnp.where(i < lat_tiles, 0, 1)
    tile = lambda w: pl.BlockSpec((tm, w), lambda i: (i, 0))
    modspec = lambda j: pl.BlockSpec((None, MOD_ROWS, d), lambda i: (l, grp(i), j))
    in_specs = (_group_specs(tm, d, lat_tiles) if len(xs) == 2 else [tile(d)]) + [
        tile(d_rnn), tile(d_rnn), *_group_specs(tm, d_fnet, lat_tiles), tile(2 * d),
        modspec(2), modspec(3), modspec(4), modspec(5),
        pl.BlockSpec((None, 1, d), lambda i: (l, 0, 0)),
    ] + [_resident(w.shape, lambda i: (0, 0)) for w in (wl, wf, wo, w1, w2)]
    args = [*xs, rec, gy, y_lat, y_ctx, gate, mod, mod, mod, mod, norm_g, wl, wf, wo, w1, w2]
    if final_g is None:
        out_specs = tile(d)
        out_shape = jax.ShapeDtypeStruct((rows, d), F32)
    else:
        in_specs.append(pl.BlockSpec((1, d), lambda i: (0, 0)))
        args.append(final_g)
        out_specs = _group_specs(tm, d, lat_tiles)
        out_shape = [
            jax.ShapeDtypeStruct((lat_tiles * tm, d), F32),
            jax.ShapeDtypeStruct(((n_tiles - lat_tiles) * tm, d), F32),
        ]
    return pl.pallas_call(
        functools.partial(_mix_ffn_kernel, d_ff, lat_tiles, final_g is not None, len(xs) == 2),
        grid=(n_tiles,),
        in_specs=in_specs,
        out_specs=out_specs,
        out_shape=out_shape,
        scratch_shapes=[pltpu.VMEM((tm, d_ff), BF16)],
        compiler_params=_params(("arbitrary",)),
        name="mix_ffn",
    )(*args)


def _cos_sin(n):
    k = np.arange(n)
    ang = 2.0 * np.pi * ((k[:, None] * k[None, :]) % n) / n
    return np.cos(ang), np.sin(ang)


def _channel_dft(d_fnet):
    gd = d_fnet // N_FNET_GROUPS
    c, s = _cos_sin(gd)
    eye = np.eye(N_FNET_GROUPS)
    scale = 1.0 / np.sqrt(gd)
    u, v = np.kron(eye, c) * scale, np.kron(eye, s) * scale
    blocks = []
    for j in range(0, d_fnet, MXU_DIM):
        blocks += [u[:, j:j + MXU_DIM], v[:, j:j + MXU_DIM]]
    return np.concatenate(blocks, axis=1)


def _row_order(seq_len):
    j = np.arange(seq_len)
    return (j % SUBLANES) * (seq_len // SUBLANES) + j // SUBLANES


def _seq_dft(n):
    c, s = _cos_sin(n)
    scale = 1.0 / np.sqrt(n)
    p = _row_order(n)
    return (c * scale)[p][:, p], (s * scale)[p][:, p]


def _grid_dft(n_rows, n_cols):
    cr, sr = _cos_sin(n_rows)
    cw, sw = _cos_sin(n_cols)
    scale = 1.0 / np.sqrt(n_rows * n_cols)
    p = _row_order(n_rows * n_cols)
    c = (np.kron(cr, cw) - np.kron(sr, sw)) * scale
    s = (np.kron(sr, cw) + np.kron(cr, sw)) * scale
    return c[p][:, p], s[p][:, p]


def _to_chunk_layout(x):
    b, l, d = x.shape
    return x.reshape(b, SUBLANES, l // SUBLANES, d).transpose(2, 0, 1, 3).reshape(b * l, d)


def _from_chunk_layout(rows, b, l):
    d = rows.shape[-1]
    return rows.reshape(l // SUBLANES, b, SUBLANES, d).transpose(1, 2, 0, 3).reshape(b, l, d)


def kernel(x_prompt, x_sample, state_lru, c, c_ctx, norm1_g, norm2_g, ada_w, ada_b, w_in, b_in, conv_w, conv_b,
           lru_wa, lru_ba, lru_wx, lru_bx, lru_lambda, w_lru_out, w_fnet_out, w_out, ffn_w_in, ffn_w_out,
           final_g):
    n_ctx, l_ctx, d = x_prompt.shape
    n_lat, l_lat, _ = x_sample.shape
    depth = w_in.shape[0]
    d_rnn = conv_w.shape[-1]
    d_fnet = w_fnet_out.shape[1]
    n_in = w_in.shape[-1]
    assert MOD_ROWS % (n_lat * SUBLANES) == 0 and MOD_ROWS % (n_ctx * SUBLANES) == 0
    assert l_lat % GRID_W == 0 and n_in == 2 * d_rnn + d_fnet + 2 * d
    rows_lat, rows_ctx = n_lat * l_lat, n_ctx * l_ctx
    rows = rows_lat + rows_ctx
    assert rows_lat % rows_ctx == 0
    lc_lat, lc_ctx = l_lat // SUBLANES, l_ctx // SUBLANES
    ctx_block = rows_lat // rows_ctx

    tm_in, tm_ffn = 1024, 512
    tm_in_first = 512

    x = (_to_chunk_layout(x_sample), _to_chunk_layout(x_prompt))

    c_pat = jnp.concatenate([
        jnp.tile(jnp.repeat(c, SUBLANES, axis=0), (MOD_ROWS // (n_lat * SUBLANES), 1)),
        jnp.broadcast_to(c_ctx[None, :], (MOD_ROWS, d))], axis=0)
    mod = _ada_call(c_pat, ada_w, ada_b)

    w_in_b = w_in[0].astype(BF16)
    wg = (0.5 * jnp.concatenate([lru_wa[:, 0], lru_wx[:, 0], lru_wa[:, 1], lru_wx[:, 1]], axis=-1)).astype(BF16)
    b_in3 = b_in.reshape(depth, 1, n_in)
    n1 = norm1_g.reshape(depth, 1, d)
    n2 = norm2_g.reshape(depth, 1, d)
    conv_b3 = conv_b.reshape(depth, 1, d_rnn)
    ba4 = lru_ba.reshape(depth, 2, 1, d_rnn)
    bx4 = lru_bx.reshape(depth, 2, 1, d_rnn)
    lam4 = lru_lambda.reshape(depth, 2, 1, d_rnn)
    h0_lat = jnp.broadcast_to(state_lru.transpose(1, 2, 0, 3)[:, :, :, None, :],
                              (depth, 2, n_lat, SUBLANES, d_rnn))
    h0_ctx = jnp.zeros((depth, 2, n_ctx, SUBLANES, d_rnn), F32)

    wc = jnp.asarray(_channel_dft(d_fnet), F32).astype(BF16)
    c_ctx_m, s_ctx_m = (jnp.asarray(m, F32).astype(BF16) for m in _seq_dft(l_ctx))
    c_lat_m, s_lat_m = (jnp.asarray(m, F32).astype(BF16) for m in _grid_dft(l_lat // GRID_W, GRID_W))

    as_lat = lambda a: a.reshape(-1, n_lat, SUBLANES, a.shape[-1])
    as_ctx = lambda a: a.reshape(-1, n_ctx, SUBLANES, a.shape[-1])

    states = []
    for l in range(depth):
        tm = tm_in_first if l == 0 else tm_in
        xr, gy, uv, gate = _inproj_call(l, x, mod, n1, w_in_b, b_in3, wc, tm, rows_lat // tm, d_rnn, d_fnet)

        rec, _, y_lat, w1_b, wl_b, wo_b = _seq_mix_call(l, 0, lc_lat, 1, as_lat(xr), as_lat(uv), conv_w, conv_b3, wg,
                                                        ba4, bx4, lam4, h0_lat, c_lat_m, s_lat_m, 64, "seq_mix_lat",
                                                        [(ffn_w_in, l), (w_lru_out, l), (w_out, l)])
        casts = [(w_fnet_out, l), (ffn_w_out, l)]
        if l + 1 < depth:
            casts.append((w_in, l + 1))
        rec, st, y_ctx, *w_b = _seq_mix_call(l, ctx_block, lc_ctx, 4, as_ctx(rec), as_ctx(uv), conv_w, conv_b3, wg,
                                             ba4, bx4, lam4, h0_ctx, c_ctx_m, s_ctx_m, 16, "seq_mix_ctx", casts)
        states.append(jnp.stack([st[0, :, SUBLANES - 1], st[1, :, 0]], axis=1))

        x = _mix_ffn_call(l, x, rec.reshape(rows, d_rnn), gy, y_lat.reshape(rows_lat, d_fnet),
                          y_ctx.reshape(rows_ctx, d_fnet), gate, mod, n2,
                          wl_b, w_b[0], wo_b, w1_b, w_b[1], tm_ffn, rows_lat // tm_ffn,
                          final_g.reshape(1, d) if l == depth - 1 else None)
        if l + 1 < depth:
            w_in_b = w_b[2]

    y_sample = _from_chunk_layout(x[0], n_lat, l_lat)
    y_prompt = _from_chunk_layout(x[1], n_ctx, l_ctx)
    new_state = jnp.stack(states, axis=1)
    return (y_prompt, y_sample, new_state)
```
